```python
import math
import jax, jax.numpy as jnp
from jax import lax
import numpy as np


D_MODEL = 1024
BATCH = 8
SEQ = 4096
DEPTH = 1

D_INNER = 2 * D_MODEL
ATTN_HEADS = 16
ATTN_HEAD_DIM = 64
D_ATTN = ATTN_HEADS * ATTN_HEAD_DIM
DILATED_PATTERNS = ((128, 1), (512, 4), (2048, 16))
ATTN_BLOCK = 128
D_SSM = D_INNER - D_ATTN
SSM_HEAD_DIM = 64
SSM_HEADS = D_SSM // SSM_HEAD_DIM
SSM_GROUPS = 2
D_STATE = 128
CONV_K = 4
CHUNK = 128
D_CONV = D_SSM + 2 * SSM_GROUPS * D_STATE
D_IN_PROJ = 4 * D_ATTN + D_SSM + D_CONV + SSM_HEADS
NORM_EPS = 1e-6
DT_MIN = 1e-3
DT_MAX = 1e-1

kernel_name = 'hymba_dilated_attn_mamba2_hybrid'


def rms_norm(x, w):
    xf = x.astype(jnp.float32)
    y = xf * lax.rsqrt(jnp.mean(xf * xf, axis=-1, keepdims=True) + NORM_EPS)
    return (y * w.astype(jnp.float32)).astype(x.dtype)


def dilated_window_attention(q, k, v, window, dilation):
    b, s, h, dh = q.shape
    n = s // dilation
    nb = -(-n // ATTN_BLOCK)
    n_pad = nb * ATTN_BLOCK
    span = window // dilation

    def to_sub(t):
        t = t.reshape(b, n, dilation, h, dh).transpose(0, 2, 3, 1, 4)
        t = jnp.pad(t, ((0, 0), (0, 0), (0, 0), (0, n_pad - n), (0, 0)))
        return t.reshape(b, dilation, h, nb, ATTN_BLOCK, dh)

    def with_prev(t):
        prev = jnp.pad(t, ((0, 0), (0, 0), (0, 0), (1, 0), (0, 0), (0, 0)))[:, :, :, :-1]
        return jnp.concatenate([prev, t], axis=4)

    qb = to_sub(q)
    kw = with_prev(to_sub(k))
    vw = with_prev(to_sub(v))
    scores = jnp.einsum('bdhnqc,bdhnkc->bdhnqk', qb, kw, preferred_element_type=jnp.float32)

    qi = jnp.arange(ATTN_BLOCK)[:, None]
    kj = jnp.arange(2 * ATTN_BLOCK)[None, :]
    dist = ATTN_BLOCK + qi - kj
    key_idx = jnp.arange(nb)[:, None, None] * ATTN_BLOCK - ATTN_BLOCK + kj[None]
    valid = (dist >= 0) & (dist <= span) & (key_idx >= 0)
    scores = jnp.where(valid, scores, -jnp.inf)

    m = jnp.max(scores, axis=-1, keepdims=True)
    p = jnp.exp(scores - m)
    l = jnp.sum(p, axis=-1, keepdims=True)
    o = jnp.einsum('bdhnqk,bdhnkc->bdhnqc', p, vw.astype(jnp.float32)) / l
    lse = (m + jnp.log(l))[..., 0]

    o = o.reshape(b, dilation, h, n_pad, dh)[:, :, :, :n].transpose(0, 3, 1, 2, 4).reshape(b, s, h, dh)
    lse = lse.reshape(b, dilation, h, n_pad)[:, :, :, :n].transpose(0, 3, 1, 2).reshape(b, s, h)
    return o, lse


def mixture_of_dilations(q, k, v):
    outs, lses = [], []
    for window, dilation in DILATED_PATTERNS:
        o, lse = dilated_window_attention(q, k, v, window, dilation)
        outs.append(o)
        lses.append(lse)
    weights = jax.nn.softmax(jnp.stack(lses), axis=0)
    return jnp.einsum('pbsh,pbshd->bshd', weights, jnp.stack(outs))


def causal_depthwise_conv(x, w, bias):
    c = x.shape[-1]
    y = lax.conv_general_dilated(
        x, w[:, None, :].astype(x.dtype), window_strides=(1,), padding=[(CONV_K - 1, 0)],
        dimension_numbers=('NWC', 'WIO', 'NWC'), feature_group_count=c)
    return y + bias.astype(x.dtype)


def ssd_chunked(x, dt, a, b_mat, c_mat):
    bsz, s, h, p = x.shape
    g = SSM_GROUPS
    e = h // g
    n = b_mat.shape[-1]
    nc = s // CHUNK
    xdt = (x.astype(jnp.float32) * dt[..., None]).reshape(bsz, nc, CHUNK, g, e, p)
    a_dt = (dt * a).reshape(bsz, nc, CHUNK, g, e).transpose(0, 3, 4, 1, 2)
    bc = b_mat.astype(jnp.float32).reshape(bsz, nc, CHUNK, g, n)
    cc = c_mat.astype(jnp.float32).reshape(bsz, nc, CHUNK, g, n)
    a_cs = jnp.cumsum(a_dt, axis=-1)

    causal = jnp.tril(jnp.ones((CHUNK, CHUNK), dtype=bool))
    seg = a_cs[..., :, None] - a_cs[..., None, :]
    decay = jnp.exp(jnp.where(causal, seg, -jnp.inf))
    cb = jnp.einsum('bclgn,bcsgn->bgcls', cc, bc)
    y_diag = jnp.einsum('bgecls,bcsgep->bclgep', cb[:, :, None] * decay, xdt)

    decay_states = jnp.exp(a_cs[..., -1:] - a_cs)
    states = jnp.einsum('bclgn,bgecl,bclgep->bcgepn', bc, decay_states, xdt)

    chunk_decay = jnp.exp(a_cs[..., -1])

    def step(h_prev, inp):
        st, dec = inp
        return h_prev * dec[..., None, None] + st, h_prev

    init = jnp.zeros((bsz, g, e, p, n), jnp.float32)
    _, prev = lax.scan(step, init, (states.transpose(1, 0, 2, 3, 4, 5), chunk_decay.transpose(3, 0, 1, 2)))
    prev = prev.transpose(1, 0, 2, 3, 4, 5)

    y_off = jnp.einsum('bclgn,bcgepn,bgecl->bclgep', cc, prev, jnp.exp(a_cs))
    return (y_diag + y_off).reshape(bsz, s, h, p)


def gated_group_rms_norm(y, z, w):
    yz = y.astype(jnp.float32) * jax.nn.silu(z.astype(jnp.float32))
    shp = yz.shape
    yz = yz.reshape(shp[:-1] + (SSM_GROUPS, shp[-1] // SSM_GROUPS))
    yz = yz * lax.rsqrt(jnp.mean(yz * yz, axis=-1, keepdims=True) + NORM_EPS)
    return yz.reshape(shp) * w.astype(jnp.float32)


def hybrid_layer(hid, norm_pre_w, w_in, conv_w, conv_b, dt_bias, a_log, d_skip, ssm_norm_w, w_out, norm_post_w):
    bsz, s, _ = hid.shape
    u = rms_norm(hid, norm_pre_w)
    proj = jnp.einsum('bsd,de->bse', u, w_in.astype(u.dtype))
    sizes = [D_ATTN, D_ATTN, D_ATTN, D_ATTN, D_SSM, D_CONV]
    q, k, v, g_attn, z, xbc, dt_raw = jnp.split(proj, np.cumsum(sizes).tolist(), axis=-1)

    q = q.reshape(bsz, s, ATTN_HEADS, ATTN_HEAD_DIM) * (ATTN_HEAD_DIM ** -0.5)
    k = k.reshape(bsz, s, ATTN_HEADS, ATTN_HEAD_DIM)
    v = v.reshape(bsz, s, ATTN_HEADS, ATTN_HEAD_DIM)
    attn = mixture_of_dilations(q, k, v).reshape(bsz, s, D_ATTN)
    attn = attn * jax.nn.silu(g_attn.astype(jnp.float32))

    xbc = jax.nn.silu(causal_depthwise_conv(xbc, conv_w, conv_b))
    xs, b_mat, c_mat = jnp.split(xbc, [D_SSM, D_SSM + SSM_GROUPS * D_STATE], axis=-1)
    xs = xs.reshape(bsz, s, SSM_HEADS, SSM_HEAD_DIM)
    b_mat = b_mat.reshape(bsz, s, SSM_GROUPS, D_STATE)
    c_mat = c_mat.reshape(bsz, s, SSM_GROUPS, D_STATE)
    dt = jax.nn.softplus(dt_raw.astype(jnp.float32) + dt_bias.astype(jnp.float32))
    a = -jnp.exp(a_log.astype(jnp.float32))
    y = ssd_chunked(xs, dt, a, b_mat, c_mat) + d_skip.astype(jnp.float32)[:, None] * xs.astype(jnp.float32)
    y = gated_group_rms_norm(y.reshape(bsz, s, D_SSM), z, ssm_norm_w)

    mix = jnp.concatenate([attn, y], axis=-1).astype(hid.dtype)
    out = jnp.einsum('bse,ed->bsd', mix, w_out.astype(mix.dtype))
    return hid + rms_norm(out, norm_post_w)


def setup_inputs(seed: int = 0) -> dict:
    key = jax.random.key(seed)
    ks = jax.random.split(key, 12)
    f32 = jnp.float32
    x = jax.random.normal(ks[0], (BATCH, SEQ, D_MODEL), f32)
    norm_pre_w = 1.0 + 0.1 * jax.random.normal(ks[1], (DEPTH, D_MODEL), f32)
    w_in = jax.random.normal(ks[2], (DEPTH, D_MODEL, D_IN_PROJ), f32) * D_MODEL ** -0.5
    conv_w = jax.random.normal(ks[3], (DEPTH, CONV_K, D_CONV), f32) * CONV_K ** -0.5
    conv_b = 0.02 * jax.random.normal(ks[4], (DEPTH, D_CONV), f32)
    dt0 = jnp.exp(jax.random.uniform(ks[5], (DEPTH, SSM_HEADS), f32, math.log(DT_MIN), math.log(DT_MAX)))
    dt_bias = dt0 + jnp.log(-jnp.expm1(-dt0))
    a_log = jnp.log(jax.random.uniform(ks[6], (DEPTH, SSM_HEADS), f32, 1.0, 16.0))
    d_skip = 1.0 + 0.1 * jax.random.normal(ks[7], (DEPTH, SSM_HEADS), f32)
    ssm_norm_w = 1.0 + 0.1 * jax.random.normal(ks[8], (DEPTH, D_SSM), f32)
    w_out = jax.random.normal(ks[9], (DEPTH, D_INNER, D_MODEL), f32) * D_INNER ** -0.5
    norm_post_w = 1.0 + 0.1 * jax.random.normal(ks[10], (DEPTH, D_MODEL), f32)
    return {'x': x, 'norm_pre_w': norm_pre_w, 'w_in': w_in, 'conv_w': conv_w, 'conv_b': conv_b,
            'dt_bias': dt_bias, 'a_log': a_log, 'd_skip': d_skip, 'ssm_norm_w': ssm_norm_w,
            'w_out': w_out, 'norm_post_w': norm_post_w}


def reference(x, norm_pre_w, w_in, conv_w, conv_b, dt_bias, a_log, d_skip, ssm_norm_w, w_out, norm_post_w):
    hid = x
    for layer in range(DEPTH):
        hid = hybrid_layer(hid, norm_pre_w[layer], w_in[layer], conv_w[layer], conv_b[layer],
                           dt_bias[layer], a_log[layer], d_skip[layer], ssm_norm_w[layer],
                           w_out[layer], norm_post_w[layer])
    return hid
```

```python
import functools

import jax
import jax.numpy as jnp
import numpy as np
from jax import lax
from jax.experimental import pallas as pl
from jax.experimental.pallas import tpu as pltpu

D_MODEL = 1024
ATTN_HEADS = 16
ATTN_HEAD_DIM = 64
D_ATTN = ATTN_HEADS * ATTN_HEAD_DIM
DILATED_PATTERNS = ((128, 1), (512, 4), (2048, 16))
ATTN_BLOCK = 128
D_SSM = 1024
SSM_HEAD_DIM = 64
SSM_HEADS = D_SSM // SSM_HEAD_DIM
SSM_GROUPS = 2
D_STATE = 128
CONV_K = 4
CHUNK = 128
D_CONV = D_SSM + 2 * SSM_GROUPS * D_STATE
D_QKVG = 4 * D_ATTN
D_IN_PROJ = D_QKVG + D_SSM + D_CONV + SSM_HEADS
NORM_EPS = 1e-6

LANES = 128
D_IN_PAD = D_QKVG + D_SSM + D_CONV + LANES
GROUP_W = D_SSM // SSM_GROUPS
HEADS_PER_GROUP = SSM_HEADS // SSM_GROUPS
MASK_VALUE = -1e30
VMEM_LIMIT = 56 * 1024 * 1024

F32 = jnp.float32
BF16 = jnp.bfloat16


def _sigmoid(x):
    return 1.0 / (1.0 + jnp.exp(-x))


def _split3(x):
    hi = x.astype(BF16)
    r1 = x - hi.astype(F32)
    mid = r1.astype(BF16)
    lo = (r1 - mid.astype(F32)).astype(BF16)
    return hi, mid, lo


IN_TM = 512
IN_CHUNK = 512


def _inproj_kernel(x_ref, nw_ref, w_ref, qkvg_ref, z_ref, xbc_ref, dt_ref):
    xf = x_ref[...]
    ms = jnp.mean(xf * xf, axis=-1, keepdims=True)
    u = (xf * lax.rsqrt(ms + NORM_EPS) * nw_ref[...]).astype(BF16)

    def proj(c0, width):
        return jnp.dot(u, w_ref[:, c0:c0 + width], preferred_element_type=F32)

    q_scale = ATTN_HEAD_DIM ** -0.5
    for c0 in range(0, D_QKVG, IN_CHUNK):
        r = proj(c0, IN_CHUNK)
        if c0 < D_ATTN:
            r = r * q_scale
        qkvg_ref[:, c0:c0 + IN_CHUNK] = r.astype(BF16)
    for c0 in range(0, D_SSM, IN_CHUNK):
        z_ref[:, c0:c0 + IN_CHUNK] = proj(D_QKVG + c0, IN_CHUNK).astype(BF16)
    for c0 in range(0, D_CONV, IN_CHUNK):
        xbc_ref[:, c0:c0 + IN_CHUNK] = proj(D_QKVG + D_SSM + c0, IN_CHUNK).astype(BF16)
    dt_ref[...] = proj(D_QKVG + D_SSM + D_CONV, LANES)


def _inproj(x2, norm_w, w_pad):
    m = x2.shape[0]
    return pl.pallas_call(
        _inproj_kernel,
        grid=(m // IN_TM,),
        in_specs=[
            pl.BlockSpec((IN_TM, D_MODEL), lambda i: (i, 0)),
            pl.BlockSpec((1, D_MODEL), lambda i: (0, 0)),
            pl.BlockSpec((D_MODEL, D_IN_PAD), lambda i: (0, 0), pipeline_mode=pl.Buffered(1)),
        ],
        out_specs=[
            pl.BlockSpec((IN_TM, D_QKVG), lambda i: (i, 0)),
            pl.BlockSpec((IN_TM, D_SSM), lambda i: (i, 0)),
            pl.BlockSpec((IN_TM, D_CONV), lambda i: (i, 0)),
            pl.BlockSpec((IN_TM, LANES), lambda i: (i, 0)),
        ],
        out_shape=[
            jax.ShapeDtypeStruct((m, D_QKVG), BF16),
            jax.ShapeDtypeStruct((m, D_SSM), BF16),
            jax.ShapeDtypeStruct((m, D_CONV), BF16),
            jax.ShapeDtypeStruct((m, LANES), F32),
        ],
        compiler_params=pltpu.CompilerParams(
            dimension_semantics=("parallel",), vmem_limit_bytes=VMEM_LIMIT),
        name="inproj",
    )(x2, norm_w, w_pad)


MAX_DIL = max(d for _, d in DILATED_PATTERNS)
MERGE_ROWS = 256


def _attn_kernel(q_ref, k_ref, v_ref, g_ref, bias_ref, o_ref,
                 qn_ref, kn_ref, vn_ref, qd_ref, kd_ref, vd_ref, *stat_refs, seq):
    acc_refs, m_refs, l_refs = stat_refs[0:3], stat_refs[3:6], stat_refs[6:9]
    blk = ATTN_BLOCK
    lane = lax.broadcasted_iota(jnp.int32, (blk, LANES), 1)
    head0 = lane < ATTN_HEAD_DIM

    qn_ref[...] = q_ref[...].astype(F32)
    kn_ref[...] = k_ref[...].astype(F32)
    vn_ref[...] = v_ref[...].astype(F32)
    vd_ref[:, LANES:] = jnp.ones((vd_ref.shape[0], LANES), BF16)

    for p, (window, dil) in enumerate(DILATED_PATTERNS):
        assert window // dil == blk
        n = seq // dil
        nb = n // blk
        pitch = n + blk
        lane_n = lax.broadcasted_iota(jnp.int32, (n, LANES), 1)
        for r in range(dil):
            base = r * pitch
            rows = slice(None) if dil == 1 else pl.ds(r, n, stride=dil)
            qf = qn_ref[rows, :]
            qd_ref[0, base + blk:base + pitch, :] = jnp.where(lane_n < ATTN_HEAD_DIM, qf, 0.0).astype(BF16)
            qd_ref[1, base + blk:base + pitch, :] = jnp.where(lane_n < ATTN_HEAD_DIM, 0.0, qf).astype(BF16)
            kd_ref[base:base + blk, :] = jnp.zeros((blk, LANES), BF16)
            kd_ref[base + blk:base + pitch, :] = kn_ref[rows, :].astype(BF16)
            vd_ref[base:base + blk, :LANES] = jnp.zeros((blk, LANES), BF16)
            vd_ref[base + blk:base + pitch, :LANES] = vn_ref[rows, :].astype(BF16)

        def block_body(t, carry, p=p, dil=dil, nb=nb, pitch=pitch):
            r = t // nb
            j = t % nb
            row0 = pl.multiple_of(r * pitch + j * blk, blk)
            kw = kd_ref[pl.ds(row0, 2 * blk), :]
            vw = vd_ref[pl.ds(row0, 2 * blk), :]
            bias = bias_ref[jnp.minimum(j, 1)]
            res = []
            for h in range(2):
                qh = qd_ref[h, pl.ds(row0 + blk, blk), :]
                s = lax.dot_general(qh, kw, (((1,), (1,)), ((), ())), preferred_element_type=F32) + bias
                m = jnp.max(s, axis=-1, keepdims=True)
                pr = jnp.exp(s - m).astype(BF16)
                oa = jnp.dot(pr, vw, preferred_element_type=F32)
                res.append((oa, jnp.broadcast_to(m, (blk, LANES))))
            acc = jnp.where(head0, res[0][0][:, :LANES], res[1][0][:, :LANES])
            den = jnp.where(head0, res[0][0][:, LANES:], res[1][0][:, LANES:])
            mx = jnp.where(head0, res[0][1], res[1][1])
            if dil == 1:
                dst = pl.ds(pl.multiple_of(j * blk, blk), blk)
            else:
                dst = pl.ds(r + dil * blk * j, blk, stride=dil)
            acc_refs[p][dst, :] = acc
            m_refs[p][dst, :] = mx
            l_refs[p][dst, :] = den
            return carry

        lax.fori_loop(0, dil * nb, block_body, 0)

    def merge_body(i, carry):
        rows = pl.ds(pl.multiple_of(i * MERGE_ROWS, MERGE_ROWS), MERGE_ROWS)
        ms = [m_refs[p][rows, :] for p in range(3)]
        mtop = jnp.maximum(jnp.maximum(ms[0], ms[1]), ms[2])
        num = jnp.zeros((MERGE_ROWS, LANES), F32)
        den = jnp.zeros((MERGE_ROWS, LANES), F32)
        for p in range(3):
            e = jnp.exp(ms[p] - mtop)
            num = num + e * acc_refs[p][rows, :]
            den = den + e * l_refs[p][rows, :]
        g = g_ref[rows, :].astype(F32)
        o_ref[rows, :] = (num / den * (g * _sigmoid(g))).astype(BF16)
        return carry

    lax.fori_loop(0, seq // MERGE_ROWS, merge_body, 0)


def _attn_bias():
    qi = np.arange(ATTN_BLOCK)[:, None]
    kj = np.arange(2 * ATTN_BLOCK)[None, :]
    dist = ATTN_BLOCK + qi - kj
    band = (dist >= 0) & (dist <= ATTN_BLOCK)
    first = band & (kj >= ATTN_BLOCK)
    bias = np.where(np.stack([first, band]), 0.0, MASK_VALUE).astype(np.float32)
    return jnp.asarray(bias)


def _attention(qkvg, bsz, seq):
    n_pairs = D_ATTN // LANES
    d_rows = seq + ATTN_BLOCK * MAX_DIL
    qkvg3 = qkvg.reshape(bsz, seq, D_QKVG)

    def col_spec(which):
        return pl.BlockSpec((None, seq, LANES), lambda b, hp: (b, 0, which * n_pairs + hp))

    scratch = [pltpu.VMEM((seq, LANES), F32)] * 3
    scratch += [pltpu.VMEM((2, d_rows, LANES), BF16), pltpu.VMEM((d_rows, LANES), BF16),
                pltpu.VMEM((d_rows, 2 * LANES), BF16)]
    scratch += [pltpu.VMEM((seq, LANES), F32)] * 9
    return pl.pallas_call(
        functools.partial(_attn_kernel, seq=seq),
        grid=(bsz, n_pairs),
        in_specs=[col_spec(0), col_spec(1), col_spec(2), col_spec(3),
                  pl.BlockSpec((2, ATTN_BLOCK, 2 * ATTN_BLOCK), lambda b, hp: (0, 0, 0))],
        out_specs=pl.BlockSpec((None, seq, LANES), lambda b, hp: (b, 0, hp)),
        out_shape=jax.ShapeDtypeStruct((bsz, seq, D_ATTN), BF16),
        scratch_shapes=scratch,
        compiler_params=pltpu.CompilerParams(
            dimension_semantics=("parallel", "parallel"), vmem_limit_bytes=VMEM_LIMIT),
        name="dilated_attn",
    )(qkvg3, qkvg3, qkvg3, qkvg3, _attn_bias())


SSD_T = 512
PREV_ROWS = 8


def _ssd_kernel(xbc_ref, prev_ref, z_ref, dt_ref, cw_ref, cb_ref, dtb_ref, alog_ref, dskip_ref,
                nw_ref, rexp_ref, tri_ref, y_ref, xpad_ref, state_ref, ydiag_ref):
    i = pl.program_id(1)

    @pl.when(i == 0)
    def _():
        state_ref[...] = jnp.zeros(state_ref.shape, F32)

    keep_prev = (i > 0).astype(F32)
    xpad_ref[0:PREV_ROWS, :] = prev_ref[...].astype(F32) * keep_prev
    xpad_ref[PREV_ROWS:, :] = xbc_ref[...].astype(F32)

    a_neg = -jnp.exp(alog_ref[...])
    li = lax.broadcasted_iota(jnp.int32, (CHUNK, CHUNK), 0)
    si = lax.broadcasted_iota(jnp.int32, (CHUNK, CHUNK), 1)
    causal = li >= si
    lane = lax.broadcasted_iota(jnp.int32, (CHUNK, LANES), 1)
    head0 = lane < SSM_HEAD_DIM
    tri = tri_ref[...]
    rexp = rexp_ref[...]

    def expand(e):
        hi = e.astype(BF16)
        lo = (e - hi.astype(F32)).astype(BF16)
        return (jnp.dot(hi, rexp, preferred_element_type=F32)
                + jnp.dot(lo, rexp, preferred_element_type=F32))

    for c in range(SSD_T // CHUNK):
        r0 = c * CHUNK
        conv = cb_ref[...]
        for kk in range(CONV_K):
            off = r0 + PREV_ROWS - (CONV_K - 1) + kk
            conv = conv + cw_ref[kk:kk + 1, :] * xpad_ref[off:off + CHUNK, :]
        xc = conv * _sigmoid(conv)
        xs = xc[:, :D_SSM]

        dt_in = dt_ref[r0:r0 + CHUNK, :] + dtb_ref[...]
        dt = jnp.maximum(dt_in, 0.0) + jnp.log1p(jnp.exp(-jnp.abs(dt_in)))
        a_dt = dt * a_neg
        a_cs = sum(jnp.dot(tri, part, preferred_element_type=F32) for part in _split3(a_dt))
        a_cs_t = a_cs.T
        a_last = a_cs[CHUNK - 1:CHUNK, :]
        exp_acs_x = expand(jnp.exp(a_cs))
        dt_x = expand(dt)
        dstate_x = expand(jnp.exp(a_last - a_cs))
        cdecay_x = exp_acs_x[CHUNK - 1:CHUNK, :]

        xdt = xs * dt_x
        xdt_b = xdt.astype(BF16)
        xdts_b = (xdt * dstate_x).astype(BF16)

        for g in range(SSM_GROUPS):
            gcols = slice(g * GROUP_W, (g + 1) * GROUP_W)
            b_f = xc[:, D_SSM + g * D_STATE:D_SSM + (g + 1) * D_STATE]
            c_b = xc[:, D_SSM + (SSM_GROUPS + g) * D_STATE:D_SSM + (SSM_GROUPS + g + 1) * D_STATE].astype(BF16)
            b_b = b_f.astype(BF16)
            cb = lax.dot_general(c_b, b_b, (((1,), (1,)), ((), ())), preferred_element_type=F32)
            s_prev = state_ref[g]
            y_off = jnp.dot(c_b, s_prev.astype(BF16), preferred_element_type=F32) * exp_acs_x[:, gcols]
            new_state = jnp.dot(b_f.T.astype(BF16), xdts_b[:, gcols], preferred_element_type=F32)
            state_ref[g] = s_prev * cdecay_x[:, gcols] + new_state
            for hp in range(HEADS_PER_GROUP // 2):
                pair = []
                pcols = slice(g * GROUP_W + hp * LANES, g * GROUP_W + (hp + 1) * LANES)
                for e in range(2):
                    h = g * HEADS_PER_GROUP + 2 * hp + e
                    seg = a_cs[:, h:h + 1] - a_cs_t[h:h + 1, :]
                    decay = jnp.exp(jnp.where(causal, seg, -jnp.inf))
                    pair.append(jnp.dot((cb * decay).astype(BF16), xdt_b[:, pcols], preferred_element_type=F32))
                ydiag_ref[:, pcols] = jnp.where(head0, pair[0], pair[1])
            y = ydiag_ref[:, gcols] + y_off + dskip_ref[:, gcols] * xs[:, gcols]
            zg = z_ref[r0:r0 + CHUNK, gcols].astype(F32)
            yz = y * (zg * _sigmoid(zg))
            ms = jnp.mean(yz * yz, axis=-1, keepdims=True)
            y_ref[r0:r0 + CHUNK, gcols] = (yz * lax.rsqrt(ms + NORM_EPS) * nw_ref[:, gcols]).astype(BF16)


def _ssd(xbc, z, dt_raw, conv_w, conv_b, dt_bias, a_log, d_skip, ssm_norm_w, bsz, seq):
    xbc3 = xbc.reshape(bsz, seq, D_CONV)
    z3 = z.reshape(bsz, seq, D_SSM)
    dt3 = dt_raw.reshape(bsz, seq, LANES)
    pad = LANES - SSM_HEADS
    dtb = jnp.pad(dt_bias, (0, pad))[None, :]
    alog = jnp.pad(a_log, (0, pad))[None, :]
    dskip = jnp.repeat(d_skip, SSM_HEAD_DIM)[None, :]
    rexp = (np.arange(LANES)[:, None] == (np.arange(D_SSM)[None, :] // SSM_HEAD_DIM))
    rexp = jnp.asarray(rexp.astype(np.float32), dtype=BF16)
    tri = jnp.asarray(np.tril(np.ones((CHUNK, CHUNK), np.float32)), dtype=BF16)
    steps_per_t = SSD_T // PREV_ROWS

    def const(shape):
        return pl.BlockSpec(shape, lambda b, i: (0,) * len(shape))

    return pl.pallas_call(
        _ssd_kernel,
        grid=(bsz, seq // SSD_T),
        in_specs=[
            pl.BlockSpec((None, SSD_T, D_CONV), lambda b, i: (b, i, 0)),
            pl.BlockSpec((None, PREV_ROWS, D_CONV), lambda b, i: (b, jnp.maximum(i * steps_per_t - 1, 0), 0)),
            pl.BlockSpec((None, SSD_T, D_SSM), lambda b, i: (b, i, 0)),
            pl.BlockSpec((None, SSD_T, LANES), lambda b, i: (b, i, 0)),
            const((CONV_K, D_CONV)), const((1, D_CONV)), const((1, LANES)), const((1, LANES)),
            const((1, D_SSM)), const((1, D_SSM)), const((LANES, D_SSM)), const((CHUNK, CHUNK)),
        ],
        out_specs=pl.BlockSpec((None, SSD_T, D_SSM), lambda b, i: (b, i, 0)),
        out_shape=jax.ShapeDtypeStruct((bsz, seq, D_SSM), BF16),
        scratch_shapes=[
            pltpu.VMEM((SSD_T + PREV_ROWS, D_CONV), F32),
            pltpu.VMEM((SSM_GROUPS, D_STATE, GROUP_W), F32),
            pltpu.VMEM((CHUNK, D_SSM), F32),
        ],
        compiler_params=pltpu.CompilerParams(
            dimension_semantics=("parallel", "arbitrary"), vmem_limit_bytes=VMEM_LIMIT),
        name="ssd",
    )(xbc3, xbc3, z3, dt3, conv_w, conv_b[None, :], dtb, alog, dskip, ssm_norm_w[None, :], rexp, tri)


OUT_TM = 512


def _outproj_kernel(attn_ref, y_ref, x_ref, w_ref, nw_ref, o_ref):
    out = jnp.dot(attn_ref[...], w_ref[:D_ATTN, :], preferred_element_type=F32)
    out = out + jnp.dot(y_ref[...], w_ref[D_ATTN:, :], preferred_element_type=F32)
    ms = jnp.mean(out * out, axis=-1, keepdims=True)
    o_ref[...] = x_ref[...] + out * lax.rsqrt(ms + NORM_EPS) * nw_ref[...]


def _outproj(attn2, y2, x2, w_out_b, norm_w):
    m = x2.shape[0]
    return pl.pallas_call(
        _outproj_kernel,
        grid=(m // OUT_TM,),
        in_specs=[
            pl.BlockSpec((OUT_TM, D_ATTN), lambda i: (i, 0)),
            pl.BlockSpec((OUT_TM, D_SSM), lambda i: (i, 0)),
            pl.BlockSpec((OUT_TM, D_MODEL), lambda i: (i, 0)),
            pl.BlockSpec((D_ATTN + D_SSM, D_MODEL), lambda i: (0, 0)),
            pl.BlockSpec((1, D_MODEL), lambda i: (0, 0)),
        ],
        out_specs=pl.BlockSpec((OUT_TM, D_MODEL), lambda i: (i, 0)),
        out_shape=jax.ShapeDtypeStruct((m, D_MODEL), F32),
        compiler_params=pltpu.CompilerParams(
            dimension_semantics=("parallel",), vmem_limit_bytes=VMEM_LIMIT),
        name="outproj",
    )(attn2, y2, x2, w_out_b, norm_w)


def _layer(hid, norm_pre_w, w_in, conv_w, conv_b, dt_bias, a_log, d_skip, ssm_norm_w, w_out, norm_post_w):
    bsz, seq, _ = hid.shape
    assert hid.shape[-1] == D_MODEL and w_in.shape == (D_MODEL, D_IN_PROJ)
    assert seq % (ATTN_BLOCK * MAX_DIL) == 0 and seq % SSD_T == 0 and (bsz * seq) % IN_TM == 0
    x2 = hid.reshape(bsz * seq, D_MODEL)
    w_pad = jnp.pad(w_in.astype(BF16), ((0, 0), (0, D_IN_PAD - D_IN_PROJ)))
    qkvg, z, xbc, dt_raw = _inproj(x2, norm_pre_w[None, :], w_pad)
    attn = _attention(qkvg, bsz, seq)
    y = _ssd(xbc, z, dt_raw, conv_w, conv_b, dt_bias, a_log, d_skip, ssm_norm_w, bsz, seq)
    out = _outproj(attn.reshape(bsz * seq, D_ATTN), y.reshape(bsz * seq, D_SSM), x2,
                   w_out.astype(BF16), norm_post_w[None, :])
    return out.reshape(bsz, seq, D_MODEL)


def kernel(x, norm_pre_w, w_in, conv_w, conv_b, dt_bias, a_log, d_skip, ssm_norm_w, w_out, norm_post_w):
    hid = x
    for layer in range(norm_pre_w.shape[0]):
        hid = _layer(hid, norm_pre_w[layer], w_in[layer], conv_w[layer], conv_b[layer],
                     dt_bias[layer], a_log[layer], d_skip[layer], ssm_norm_w[layer],
                     w_out[layer], norm_post_w[layer])
    return hid
```

```python
import functools

import jax
import jax.numpy as jnp
import numpy as np
from jax import lax
from jax.experimental import pallas as pl
from jax.experimental.pallas import tpu as pltpu

D_MODEL = 1024
ATTN_HEADS = 16
ATTN_HEAD_DIM = 64
D_ATTN = ATTN_HEADS * ATTN_HEAD_DIM
DILATED_PATTERNS = ((128, 1), (512, 4), (2048, 16))
ATTN_BLOCK = 128
D_SSM = 1024
SSM_HEAD_DIM = 64
SSM_HEADS = D_SSM // SSM_HEAD_DIM
SSM_GROUPS = 2
D_STATE = 128
CONV_K = 4
CHUNK = 128
D_CONV = D_SSM + 2 * SSM_GROUPS * D_STATE
D_QKVG = 4 * D_ATTN
D_IN_PROJ = D_QKVG + D_SSM + D_CONV + SSM_HEADS
NORM_EPS = 1e-6

LANES = 128
D_IN_PAD = D_QKVG + D_SSM + D_CONV + LANES
GROUP_W = D_SSM // SSM_GROUPS
HEADS_PER_GROUP = SSM_HEADS // SSM_GROUPS
MASK_VALUE = -1e30
VMEM_LIMIT = 56 * 1024 * 1024

F32 = jnp.float32
BF16 = jnp.bfloat16


def _sigmoid(x):
    return 1.0 / (1.0 + jnp.exp(-x))


def _split3(x):
    hi = x.astype(BF16)
    r1 = x - hi.astype(F32)
    mid = r1.astype(BF16)
    lo = (r1 - mid.astype(F32)).astype(BF16)
    return hi, mid, lo


IN_TM = 512
IN_CHUNK = 512


def _inproj_kernel(x_ref, nw_ref, w_ref, qkvg_ref, z_ref, xbc_ref, dt_ref):
    xf = x_ref[...]
    ms = jnp.mean(xf * xf, axis=-1, keepdims=True)
    u = (xf * lax.rsqrt(ms + NORM_EPS) * nw_ref[...]).astype(BF16)

    def proj(c0, width):
        return jnp.dot(u, w_ref[:, c0:c0 + width], preferred_element_type=F32)

    q_scale = ATTN_HEAD_DIM ** -0.5
    for c0 in range(0, D_QKVG, IN_CHUNK):
        r = proj(c0, IN_CHUNK)
        if c0 < D_ATTN:
            r = r * q_scale
        qkvg_ref[:, c0:c0 + IN_CHUNK] = r.astype(BF16)
    for c0 in range(0, D_SSM, IN_CHUNK):
        z_ref[:, c0:c0 + IN_CHUNK] = proj(D_QKVG + c0, IN_CHUNK).astype(BF16)
    for c0 in range(0, D_CONV, IN_CHUNK):
        xbc_ref[:, c0:c0 + IN_CHUNK] = proj(D_QKVG + D_SSM + c0, IN_CHUNK).astype(BF16)
    dt_ref[...] = proj(D_QKVG + D_SSM + D_CONV, LANES)


def _inproj(x2, norm_w, w_pad):
    m = x2.shape[0]
    return pl.pallas_call(
        _inproj_kernel,
        grid=(m // IN_TM,),
        in_specs=[
            pl.BlockSpec((IN_TM, D_MODEL), lambda i: (i, 0)),
            pl.BlockSpec((1, D_MODEL), lambda i: (0, 0)),
            pl.BlockSpec((D_MODEL, D_IN_PAD), lambda i: (0, 0), pipeline_mode=pl.Buffered(1)),
        ],
        out_specs=[
            pl.BlockSpec((IN_TM, D_QKVG), lambda i: (i, 0)),
            pl.BlockSpec((IN_TM, D_SSM), lambda i: (i, 0)),
            pl.BlockSpec((IN_TM, D_CONV), lambda i: (i, 0)),
            pl.BlockSpec((IN_TM, LANES), lambda i: (i, 0)),
        ],
        out_shape=[
            jax.ShapeDtypeStruct((m, D_QKVG), BF16),
            jax.ShapeDtypeStruct((m, D_SSM), BF16),
            jax.ShapeDtypeStruct((m, D_CONV), BF16),
            jax.ShapeDtypeStruct((m, LANES), F32),
        ],
        compiler_params=pltpu.CompilerParams(
            dimension_semantics=("parallel",), vmem_limit_bytes=VMEM_LIMIT),
        name="inproj",
    )(x2, norm_w, w_pad)


MAX_DIL = max(d for _, d in DILATED_PATTERNS)
MERGE_ROWS = 256


def _attn_kernel(q_ref, k_ref, v_ref, g_ref, bias_ref, o_ref,
                 qn_ref, kn_ref, vn_ref, qd_ref, kd_ref, vd_ref, *stat_refs, seq):
    acc_refs, m_refs, l_refs = stat_refs[0:3], stat_refs[3:6], stat_refs[6:9]
    blk = ATTN_BLOCK
    lane = lax.broadcasted_iota(jnp.int32, (blk, LANES), 1)
    head0 = lane < ATTN_HEAD_DIM

    qn_ref[...] = q_ref[...].astype(F32)
    kn_ref[...] = k_ref[...].astype(F32)
    vn_ref[...] = v_ref[...].astype(F32)
    vd_ref[:, LANES:] = jnp.ones((vd_ref.shape[0], LANES), BF16)

    for p, (window, dil) in enumerate(DILATED_PATTERNS):
        assert window // dil == blk
        n = seq // dil
        nb = n // blk
        pitch = n + blk
        lane_n = lax.broadcasted_iota(jnp.int32, (n, LANES), 1)
        for r in range(dil):
            base = r * pitch
            rows = slice(None) if dil == 1 else pl.ds(r, n, stride=dil)
            qf = qn_ref[rows, :]
            qd_ref[0, base + blk:base + pitch, :] = jnp.where(lane_n < ATTN_HEAD_DIM, qf, 0.0).astype(BF16)
            qd_ref[1, base + blk:base + pitch, :] = jnp.where(lane_n < ATTN_HEAD_DIM, 0.0, qf).astype(BF16)
            kd_ref[base:base + blk, :] = jnp.zeros((blk, LANES), BF16)
            kd_ref[base + blk:base + pitch, :] = kn_ref[rows, :].astype(BF16)
            vd_ref[base:base + blk, :LANES] = jnp.zeros((blk, LANES), BF16)
            vd_ref[base + blk:base + pitch, :LANES] = vn_ref[rows, :].astype(BF16)

        def block_body(t, carry, p=p, dil=dil, nb=nb, pitch=pitch):
            r = t // nb
            j = t % nb
            row0 = pl.multiple_of(r * pitch + j * blk, blk)
            kw = kd_ref[pl.ds(row0, 2 * blk), :]
            vw = vd_ref[pl.ds(row0, 2 * blk), :]
            bias = bias_ref[jnp.minimum(j, 1)]
            res = []
            for h in range(2):
                qh = qd_ref[h, pl.ds(row0 + blk, blk), :]
                s = lax.dot_general(qh, kw, (((1,), (1,)), ((), ())), preferred_element_type=F32) + bias
                m = jnp.max(s, axis=-1, keepdims=True)
                pr = jnp.exp(s - m).astype(BF16)
                oa = jnp.dot(pr, vw, preferred_element_type=F32)
                res.append((oa, jnp.broadcast_to(m, (blk, LANES))))
            acc = jnp.where(head0, res[0][0][:, :LANES], res[1][0][:, :LANES])
            den = jnp.where(head0, res[0][0][:, LANES:], res[1][0][:, LANES:])
            mx = jnp.where(head0, res[0][1], res[1][1])
            if dil == 1:
                dst = pl.ds(pl.multiple_of(j * blk, blk), blk)
            else:
                dst = pl.ds(r + dil * blk * j, blk, stride=dil)
            acc_refs[p][dst, :] = acc
            m_refs[p][dst, :] = mx
            l_refs[p][dst, :] = den
            return carry

        lax.fori_loop(0, dil * nb, block_body, 0, unroll=8)

    def merge_body(i, carry):
        rows = pl.ds(pl.multiple_of(i * MERGE_ROWS, MERGE_ROWS), MERGE_ROWS)
        ms = [m_refs[p][rows, :] for p in range(3)]
        mtop = jnp.maximum(jnp.maximum(ms[0], ms[1]), ms[2])
        num = jnp.zeros((MERGE_ROWS, LANES), F32)
        den = jnp.zeros((MERGE_ROWS, LANES), F32)
        for p in range(3):
            e = jnp.exp(ms[p] - mtop)
            num = num + e * acc_refs[p][rows, :]
            den = den + e * l_refs[p][rows, :]
        g = g_ref[rows, :].astype(F32)
        o_ref[rows, :] = (num / den * (g * _sigmoid(g))).astype(BF16)
        return carry

    lax.fori_loop(0, seq // MERGE_ROWS, merge_body, 0)


def _attn_bias():
    qi = np.arange(ATTN_BLOCK)[:, None]
    kj = np.arange(2 * ATTN_BLOCK)[None, :]
    dist = ATTN_BLOCK + qi - kj
    band = (dist >= 0) & (dist <= ATTN_BLOCK)
    first = band & (kj >= ATTN_BLOCK)
    bias = np.where(np.stack([first, band]), 0.0, MASK_VALUE).astype(np.float32)
    return jnp.asarray(bias)


def _attention(qkvg, bsz, seq):
    n_pairs = D_ATTN // LANES
    d_rows = seq + ATTN_BLOCK * MAX_DIL
    qkvg3 = qkvg.reshape(bsz, seq, D_QKVG)

    def col_spec(which):
        return pl.BlockSpec((None, seq, LANES), lambda b, hp: (b, 0, which * n_pairs + hp))

    scratch = [pltpu.VMEM((seq, LANES), F32)] * 3
    scratch += [pltpu.VMEM((2, d_rows, LANES), BF16), pltpu.VMEM((d_rows, LANES), BF16),
                pltpu.VMEM((d_rows, 2 * LANES), BF16)]
    scratch += [pltpu.VMEM((seq, LANES), F32)] * 9
    return pl.pallas_call(
        functools.partial(_attn_kernel, seq=seq),
        grid=(bsz, n_pairs),
        in_specs=[col_spec(0), col_spec(1), col_spec(2), col_spec(3),
                  pl.BlockSpec((2, ATTN_BLOCK, 2 * ATTN_BLOCK), lambda b, hp: (0, 0, 0))],
        out_specs=pl.BlockSpec((None, seq, LANES), lambda b, hp: (b, 0, hp)),
        out_shape=jax.ShapeDtypeStruct((bsz, seq, D_ATTN), BF16),
        scratch_shapes=scratch,
        compiler_params=pltpu.CompilerParams(
            dimension_semantics=("parallel", "parallel"), vmem_limit_bytes=VMEM_LIMIT),
        name="dilated_attn",
    )(qkvg3, qkvg3, qkvg3, qkvg3, _attn_bias())


SSD_T = 512
PREV_ROWS = 8


def _ssd_kernel(xbc_ref, prev_ref, z_ref, dt_ref, cw_ref, cb_ref, dtb_ref, alog_ref, dskip_ref,
                nw_ref, rexp_ref, tri_ref, y_ref, xpad_ref, state_ref, ydiag_ref):
    i = pl.program_id(1)

    @pl.when(i == 0)
    def _():
        state_ref[...] = jnp.zeros(state_ref.shape, F32)

    keep_prev = (i > 0).astype(F32)
    xpad_ref[0:PREV_ROWS, :] = prev_ref[...].astype(F32) * keep_prev
    xpad_ref[PREV_ROWS:, :] = xbc_ref[...].astype(F32)

    a_neg = -jnp.exp(alog_ref[...])
    li = lax.broadcasted_iota(jnp.int32, (CHUNK, CHUNK), 0)
    si = lax.broadcasted_iota(jnp.int32, (CHUNK, CHUNK), 1)
    causal = li >= si
    lane = lax.broadcasted_iota(jnp.int32, (CHUNK, LANES), 1)
    head0 = lane < SSM_HEAD_DIM
    tri = tri_ref[...]
    rexp = rexp_ref[...]

    def expand(e):
        hi = e.astype(BF16)
        lo = (e - hi.astype(F32)).astype(BF16)
        return (jnp.dot(hi, rexp, preferred_element_type=F32)
                + jnp.dot(lo, rexp, preferred_element_type=F32))

    for c in range(SSD_T // CHUNK):
        r0 = c * CHUNK
        conv = cb_ref[...]
        for kk in range(CONV_K):
            off = r0 + PREV_ROWS - (CONV_K - 1) + kk
            conv = conv + cw_ref[kk:kk + 1, :] * xpad_ref[off:off + CHUNK, :]
        xc = conv * _sigmoid(conv)
        xs = xc[:, :D_SSM]

        dt_in = dt_ref[r0:r0 + CHUNK, :] + dtb_ref[...]
        dt = jnp.maximum(dt_in, 0.0) + jnp.log1p(jnp.exp(-jnp.abs(dt_in)))
        a_dt = dt * a_neg
        a_cs = sum(jnp.dot(tri, part, preferred_element_type=F32) for part in _split3(a_dt))
        a_cs_t = a_cs.T
        a_last = a_cs[CHUNK - 1:CHUNK, :]
        exp_acs_x = expand(jnp.exp(a_cs))
        dt_x = expand(dt)
        dstate_x = expand(jnp.exp(a_last - a_cs))
        cdecay_x = exp_acs_x[CHUNK - 1:CHUNK, :]

        xdt = xs * dt_x
        xdt_b = xdt.astype(BF16)
        xdts_b = (xdt * dstate_x).astype(BF16)

        for g in range(SSM_GROUPS):
            gcols = slice(g * GROUP_W, (g + 1) * GROUP_W)
            b_f = xc[:, D_SSM + g * D_STATE:D_SSM + (g + 1) * D_STATE]
            c_b = xc[:, D_SSM + (SSM_GROUPS + g) * D_STATE:D_SSM + (SSM_GROUPS + g + 1) * D_STATE].astype(BF16)
            b_b = b_f.astype(BF16)
            cb = lax.dot_general(c_b, b_b, (((1,), (1,)), ((), ())), preferred_element_type=F32)
            s_prev = state_ref[g]
            y_off = jnp.dot(c_b, s_prev.astype(BF16), preferred_element_type=F32) * exp_acs_x[:, gcols]
            new_state = jnp.dot(b_f.T.astype(BF16), xdts_b[:, gcols], preferred_element_type=F32)
            state_ref[g] = s_prev * cdecay_x[:, gcols] + new_state
            for hp in range(HEADS_PER_GROUP // 2):
                pair = []
                pcols = slice(g * GROUP_W + hp * LANES, g * GROUP_W + (hp + 1) * LANES)
                for e in range(2):
                    h = g * HEADS_PER_GROUP + 2 * hp + e
                    seg = a_cs[:, h:h + 1] - a_cs_t[h:h + 1, :]
                    decay = jnp.exp(jnp.where(causal, seg, -jnp.inf))
                    pair.append(jnp.dot((cb * decay).astype(BF16), xdt_b[:, pcols], preferred_element_type=F32))
                ydiag_ref[:, pcols] = jnp.where(head0, pair[0], pair[1])
            y = ydiag_ref[:, gcols] + y_off + dskip_ref[:, gcols] * xs[:, gcols]
            zg = z_ref[r0:r0 + CHUNK, gcols].astype(F32)
            yz = y * (zg * _sigmoid(zg))
            ms = jnp.mean(yz * yz, axis=-1, keepdims=True)
            y_ref[r0:r0 + CHUNK, gcols] = (yz * lax.rsqrt(ms + NORM_EPS) * nw_ref[:, gcols]).astype(BF16)


def _ssd(xbc, z, dt_raw, conv_w, conv_b, dt_bias, a_log, d_skip, ssm_norm_w, bsz, seq):
    xbc3 = xbc.reshape(bsz, seq, D_CONV)
    z3 = z.reshape(bsz, seq, D_SSM)
    dt3 = dt_raw.reshape(bsz, seq, LANES)
    pad = LANES - SSM_HEADS
    dtb = jnp.pad(dt_bias, (0, pad))[None, :]
    alog = jnp.pad(a_log, (0, pad))[None, :]
    dskip = jnp.repeat(d_skip, SSM_HEAD_DIM)[None, :]
    rexp = (np.arange(LANES)[:, None] == (np.arange(D_SSM)[None, :] // SSM_HEAD_DIM))
    rexp = jnp.asarray(rexp.astype(np.float32), dtype=BF16)
    tri = jnp.asarray(np.tril(np.ones((CHUNK, CHUNK), np.float32)), dtype=BF16)
    steps_per_t = SSD_T // PREV_ROWS

    def const(shape):
        return pl.BlockSpec(shape, lambda b, i: (0,) * len(shape))

    return pl.pallas_call(
        _ssd_kernel,
        grid=(bsz, seq // SSD_T),
        in_specs=[
            pl.BlockSpec((None, SSD_T, D_CONV), lambda b, i: (b, i, 0)),
            pl.BlockSpec((None, PREV_ROWS, D_CONV), lambda b, i: (b, jnp.maximum(i * steps_per_t - 1, 0), 0)),
            pl.BlockSpec((None, SSD_T, D_SSM), lambda b, i: (b, i, 0)),
            pl.BlockSpec((None, SSD_T, LANES), lambda b, i: (b, i, 0)),
            const((CONV_K, D_CONV)), const((1, D_CONV)), const((1, LANES)), const((1, LANES)),
            const((1, D_SSM)), const((1, D_SSM)), const((LANES, D_SSM)), const((CHUNK, CHUNK)),
        ],
        out_specs=pl.BlockSpec((None, SSD_T, D_SSM), lambda b, i: (b, i, 0)),
        out_shape=jax.ShapeDtypeStruct((bsz, seq, D_SSM), BF16),
        scratch_shapes=[
            pltpu.VMEM((SSD_T + PREV_ROWS, D_CONV), F32),
            pltpu.VMEM((SSM_GROUPS, D_STATE, GROUP_W), F32),
            pltpu.VMEM((CHUNK, D_SSM), F32),
        ],
        compiler_params=pltpu.CompilerParams(
            dimension_semantics=("parallel", "arbitrary"), vmem_limit_bytes=VMEM_LIMIT),
        name="ssd",
    )(xbc3, xbc3, z3, dt3, conv_w, conv_b[None, :], dtb, alog, dskip, ssm_norm_w[None, :], rexp, tri)


OUT_TM = 512


def _outproj_kernel(attn_ref, y_ref, x_ref, w_ref, nw_ref, o_ref):
    out = jnp.dot(attn_ref[...], w_ref[:D_ATTN, :], preferred_element_type=F32)
    out = out + jnp.dot(y_ref[...], w_ref[D_ATTN:, :], preferred_element_type=F32)
    ms = jnp.mean(out * out, axis=-1, keepdims=True)
    o_ref[...] = x_ref[...] + out * lax.rsqrt(ms + NORM_EPS) * nw_ref[...]


def _outproj(attn2, y2, x2, w_out_b, norm_w):
    m = x2.shape[0]
    return pl.pallas_call(
        _outproj_kernel,
        grid=(m // OUT_TM,),
        in_specs=[
            pl.BlockSpec((OUT_TM, D_ATTN), lambda i: (i, 0)),
            pl.BlockSpec((OUT_TM, D_SSM), lambda i: (i, 0)),
            pl.BlockSpec((OUT_TM, D_MODEL), lambda i: (i, 0)),
            pl.BlockSpec((D_ATTN + D_SSM, D_MODEL), lambda i: (0, 0)),
            pl.BlockSpec((1, D_MODEL), lambda i: (0, 0)),
        ],
        out_specs=pl.BlockSpec((OUT_TM, D_MODEL), lambda i: (i, 0)),
        out_shape=jax.ShapeDtypeStruct((m, D_MODEL), F32),
        compiler_params=pltpu.CompilerParams(
            dimension_semantics=("parallel",), vmem_limit_bytes=VMEM_LIMIT),
        name="outproj",
    )(attn2, y2, x2, w_out_b, norm_w)


def _layer(hid, norm_pre_w, w_in, conv_w, conv_b, dt_bias, a_log, d_skip, ssm_norm_w, w_out, norm_post_w):
    bsz, seq, _ = hid.shape
    assert hid.shape[-1] == D_MODEL and w_in.shape == (D_MODEL, D_IN_PROJ)
    assert seq % (ATTN_BLOCK * MAX_DIL) == 0 and seq % SSD_T == 0 and (bsz * seq) % IN_TM == 0
    x2 = hid.reshape(bsz * seq, D_MODEL)
    w_pad = jnp.pad(w_in.astype(BF16), ((0, 0), (0, D_IN_PAD - D_IN_PROJ)))
    qkvg, z, xbc, dt_raw = _inproj(x2, norm_pre_w[None, :], w_pad)
    attn = _attention(qkvg, bsz, seq)
    y = _ssd(xbc, z, dt_raw, conv_w, conv_b, dt_bias, a_log, d_skip, ssm_norm_w, bsz, seq)
    out = _outproj(attn.reshape(bsz * seq, D_ATTN), y.reshape(bsz * seq, D_SSM), x2,
                   w_out.astype(BF16), norm_post_w[None, :])
    return out.reshape(bsz, seq, D_MODEL)


def kernel(x, norm_pre_w, w_in, conv_w, conv_b, dt_bias, a_log, d_skip, ssm_norm_w, w_out, norm_post_w):
    hid = x
    for layer in range(norm_pre_w.shape[0]):
        hid = _layer(hid, norm_pre_w[layer], w_in[layer], conv_w[layer], conv_b[layer],
                     dt_bias[layer], a_log[layer], d_skip[layer], ssm_norm_w[layer],
                     w_out[layer], norm_post_w[layer])
    return hid
```

```python
import functools

import jax
import jax.numpy as jnp
import numpy as np
from jax import lax
from jax.experimental import pallas as pl
from jax.experimental.pallas import tpu as pltpu

D_MODEL = 1024
ATTN_HEADS = 16
ATTN_HEAD_DIM = 64
D_ATTN = ATTN_HEADS * ATTN_HEAD_DIM
DILATED_PATTERNS = ((128, 1), (512, 4), (2048, 16))
ATTN_BLOCK = 128
D_SSM = 1024
SSM_HEAD_DIM = 64
SSM_HEADS = D_SSM // SSM_HEAD_DIM
SSM_GROUPS = 2
D_STATE = 128
CONV_K = 4
CHUNK = 128
D_CONV = D_SSM + 2 * SSM_GROUPS * D_STATE
D_QKVG = 4 * D_ATTN
D_IN_PROJ = D_QKVG + D_SSM + D_CONV + SSM_HEADS
NORM_EPS = 1e-6

LANES = 128
SUBLANES = 8
D_SSM_PAD = D_SSM + D_CONV + LANES
GROUP_W = D_SSM // SSM_GROUPS
HEADS_PER_GROUP = SSM_HEADS // SSM_GROUPS
MASK_VALUE = -1e30
VMEM_LIMIT = 56 * 1024 * 1024

NSEG = max(d for _, d in DILATED_PATTERNS)

F32 = jnp.float32
BF16 = jnp.bfloat16


def _sigmoid(x):
    return 1.0 / (1.0 + jnp.exp(-x))


def _split3(x):
    hi = x.astype(BF16)
    r1 = x - hi.astype(F32)
    mid = r1.astype(BF16)
    lo = (r1 - mid.astype(F32)).astype(BF16)
    return hi, mid, lo


IN_CHUNK = 512
IN_TM = 512
IN_ROWS = IN_TM // NSEG


def _inproj_kernel(x_ref, nw_ref, wa_ref, ws_ref, qkvg_ref, z_ref, xbc_ref, dt_ref, slab_ref, ur_ref):
    xf = x_ref[...]
    ms = jnp.mean(xf * xf, axis=-1, keepdims=True)
    uf = xf * lax.rsqrt(ms + NORM_EPS) * nw_ref[...]
    un = uf.astype(BF16)

    n_slabs = D_MODEL // LANES
    for c in range(n_slabs):
        slab_ref[c] = uf[:, c * LANES:(c + 1) * LANES]

    def proj(c0, width):
        return jnp.dot(un, ws_ref[:, c0:c0 + width], preferred_element_type=F32)

    for c0 in range(0, D_SSM, IN_CHUNK):
        z_ref[:, c0:c0 + IN_CHUNK] = proj(c0, IN_CHUNK).astype(BF16)
    for c0 in range(0, D_CONV, IN_CHUNK):
        xbc_ref[:, c0:c0 + IN_CHUNK] = proj(D_SSM + c0, IN_CHUNK).astype(BF16)
    dt_ref[...] = proj(D_SSM + D_CONV, LANES)

    for c in range(n_slabs):
        for r in range(NSEG):
            ur_ref[r * IN_ROWS:(r + 1) * IN_ROWS, c * LANES:(c + 1) * LANES] = (
                slab_ref[c, pl.ds(r, IN_ROWS, stride=NSEG), :].astype(BF16))
    ur = ur_ref[...]

    q_scale = ATTN_HEAD_DIM ** -0.5
    for c0 in range(0, D_QKVG, IN_CHUNK):
        res = jnp.dot(ur, wa_ref[:, c0:c0 + IN_CHUNK], preferred_element_type=F32)
        if c0 < D_ATTN:
            res = res * q_scale
        res = res.astype(BF16)
        for r in range(NSEG):
            qkvg_ref[r, :, c0:c0 + IN_CHUNK] = res[r * IN_ROWS:(r + 1) * IN_ROWS, :]


def _inproj(x, norm_w, w_attn, w_ssm):
    bsz, seq, _ = x.shape
    seg_len = seq // NSEG
    return pl.pallas_call(
        _inproj_kernel,
        grid=(bsz, seq // IN_TM),
        in_specs=[
            pl.BlockSpec((None, IN_TM, D_MODEL), lambda b, k: (b, k, 0)),
            pl.BlockSpec((1, D_MODEL), lambda b, k: (0, 0)),
            pl.BlockSpec((D_MODEL, D_QKVG), lambda b, k: (0, 0), pipeline_mode=pl.Buffered(1)),
            pl.BlockSpec((D_MODEL, D_SSM_PAD), lambda b, k: (0, 0), pipeline_mode=pl.Buffered(1)),
        ],
        out_specs=[
            pl.BlockSpec((None, NSEG, IN_ROWS, D_QKVG), lambda b, k: (b, 0, k, 0)),
            pl.BlockSpec((None, IN_TM, D_SSM), lambda b, k: (b, k, 0)),
            pl.BlockSpec((None, IN_TM, D_CONV), lambda b, k: (b, k, 0)),
            pl.BlockSpec((None, IN_TM, LANES), lambda b, k: (b, k, 0)),
        ],
        out_shape=[
            jax.ShapeDtypeStruct((bsz, NSEG, seg_len, D_QKVG), BF16),
            jax.ShapeDtypeStruct((bsz, seq, D_SSM), BF16),
            jax.ShapeDtypeStruct((bsz, seq, D_CONV), BF16),
            jax.ShapeDtypeStruct((bsz, seq, LANES), F32),
        ],
        scratch_shapes=[
            pltpu.VMEM((D_MODEL // LANES, IN_TM, LANES), F32),
            pltpu.VMEM((IN_TM, D_MODEL), BF16),
        ],
        compiler_params=pltpu.CompilerParams(
            dimension_semantics=("parallel", "parallel"), vmem_limit_bytes=VMEM_LIMIT),
        name="inproj",
    )(x, norm_w, w_attn, w_ssm)


P1_UNROLL = 8
P2_UNROLL = 8
P3_UNROLL = 4
OUT_ROWS = 512


def _attn_kernel(q_ref, k_ref, v_ref, g_ref, bias_ref, o_ref,
                 qf_ref, kf_ref, vf_ref, qm_ref, onat_ref, *stat_refs):
    acc_refs, m_refs, l_refs = stat_refs[0:3], stat_refs[3:6], stat_refs[6:9]
    nseg, seg_len, _ = q_ref.shape
    blk = ATTN_BLOCK
    head0 = lax.broadcasted_iota(jnp.int32, (blk, LANES), 1) < ATTN_HEAD_DIM
    head0_seg = lax.broadcasted_iota(jnp.int32, (seg_len, LANES), 1) < ATTN_HEAD_DIM

    def prep(r, carry):
        qf = q_ref[r].astype(F32)
        qf_ref[r] = qf
        qm_ref[0, r] = jnp.where(head0_seg, qf, 0.0).astype(BF16)
        qm_ref[1, r] = jnp.where(head0_seg, 0.0, qf).astype(BF16)
        kf_ref[r] = k_ref[r].astype(F32)
        vf_ref[r] = v_ref[r].astype(F32)
        return carry

    lax.fori_loop(0, nseg, prep, 0)

    def attend(q01, kw, vw, bias):
        nq, nk = bias.shape
        s = lax.dot_general(q01, kw, (((1,), (1,)), ((), ())), preferred_element_type=F32)
        s = s + jnp.concatenate([bias, bias], axis=0)
        m = jnp.max(s, axis=-1, keepdims=True)
        pr = jnp.exp(s - m).astype(BF16)
        vw1 = jnp.concatenate([vw, jnp.ones((nk, LANES), BF16)], axis=1)
        oa = jnp.dot(pr, vw1, preferred_element_type=F32)
        mb = jnp.broadcast_to(m, (2 * nq, LANES))
        outs = []
        for i0 in range(0, nq, blk):
            lo, hi = slice(i0, i0 + blk), slice(nq + i0, nq + i0 + blk)
            outs.append((jnp.where(head0, oa[lo, :LANES], oa[hi, :LANES]),
                         jnp.where(head0, mb[lo], mb[hi]),
                         jnp.where(head0, oa[lo, LANES:], oa[hi, LANES:])))
        return outs

    def p1_body(j, carry):
        iq = pl.multiple_of(j * SUBLANES, SUBLANES)
        ik = pl.multiple_of(jnp.maximum(j - 1, 0) * SUBLANES, SUBLANES)
        qf = jnp.concatenate([qf_ref[r, pl.ds(iq, SUBLANES), :] for r in range(nseg)], axis=0)
        q01 = jnp.concatenate([jnp.where(head0, qf, 0.0), jnp.where(head0, 0.0, qf)], axis=0).astype(BF16)
        kw = jnp.concatenate([kf_ref[r, pl.ds(ik, 2 * SUBLANES), :] for r in range(nseg)], axis=0).astype(BF16)
        vw = jnp.concatenate([vf_ref[r, pl.ds(ik, 2 * SUBLANES), :] for r in range(nseg)], axis=0).astype(BF16)
        (acc, mx, den), = attend(q01, kw, vw, bias_ref[0, jnp.minimum(j, 1)])
        for r in range(nseg):
            rows = slice(r * SUBLANES, (r + 1) * SUBLANES)
            acc_refs[0][r, pl.ds(iq, SUBLANES), :] = acc[rows]
            m_refs[0][r, pl.ds(iq, SUBLANES), :] = mx[rows]
            l_refs[0][r, pl.ds(iq, SUBLANES), :] = den[rows]
        return carry

    lax.fori_loop(0, seg_len // SUBLANES, p1_body, 0, unroll=P1_UNROLL)

    dil2 = DILATED_PATTERNS[1][1]
    na = nseg // dil2
    cq = blk // na
    nb2 = seg_len // cq

    def p2_body(t, carry):
        r4 = t // nb2
        j = t % nb2
        iq = pl.multiple_of(j * cq, cq)
        ik = pl.multiple_of(jnp.maximum(j - 1, 0) * cq, cq)
        segs = [dil2 * a + r4 for a in range(na)]
        q01 = jnp.concatenate([qm_ref[h, sg, pl.ds(iq, cq), :] for h in range(2) for sg in segs], axis=0)
        kw = jnp.concatenate([k_ref[sg, pl.ds(ik, 2 * cq), :] for sg in segs], axis=0)
        vw = jnp.concatenate([v_ref[sg, pl.ds(ik, 2 * cq), :] for sg in segs], axis=0)
        (acc, mx, den), = attend(q01, kw, vw, bias_ref[1, jnp.minimum(j, 1)])
        for a, sg in enumerate(segs):
            rows = slice(a * cq, (a + 1) * cq)
            acc_refs[1][sg, pl.ds(iq, cq), :] = acc[rows]
            m_refs[1][sg, pl.ds(iq, cq), :] = mx[rows]
            l_refs[1][sg, pl.ds(iq, cq), :] = den[rows]
        return carry

    lax.fori_loop(0, dil2 * nb2, p2_body, 0, unroll=P2_UNROLL)

    assert seg_len == 2 * blk and DILATED_PATTERNS[2][1] == nseg

    def p3_body(r, carry):
        q01 = jnp.concatenate([qm_ref[0, r], qm_ref[1, r]], axis=0)
        bias = jnp.concatenate([bias_ref[2, 0], bias_ref[2, 1]], axis=0)
        outs = attend(q01, k_ref[r], v_ref[r], bias)
        for jb, (acc, mx, den) in enumerate(outs):
            rows = slice(jb * blk, (jb + 1) * blk)
            acc_refs[2][r, rows, :] = acc
            m_refs[2][r, rows, :] = mx
            l_refs[2][r, rows, :] = den
        return carry

    lax.fori_loop(0, nseg, p3_body, 0, unroll=P3_UNROLL)

    def merge_body(r, carry):
        ms = [m_refs[p][r] for p in range(3)]
        mtop = jnp.maximum(jnp.maximum(ms[0], ms[1]), ms[2])
        num = jnp.zeros((seg_len, LANES), F32)
        den = jnp.zeros((seg_len, LANES), F32)
        for p in range(3):
            e = jnp.exp(ms[p] - mtop)
            num = num + e * acc_refs[p][r]
            den = den + e * l_refs[p][r]
        g = g_ref[r].astype(F32)
        onat_ref[pl.ds(r, seg_len, stride=nseg), :] = num / den * (g * _sigmoid(g))
        return carry

    lax.fori_loop(0, nseg, merge_body, 0)

    def out_body(i, carry):
        rows = pl.ds(pl.multiple_of(i * OUT_ROWS, OUT_ROWS), OUT_ROWS)
        o_ref[rows, :] = onat_ref[rows, :].astype(BF16)
        return carry

    lax.fori_loop(0, nseg * seg_len // OUT_ROWS, out_body, 0)


def _attn_bias(seg_len):
    blk = ATTN_BLOCK
    tables = []
    for _, dil in DILATED_PATTERNS:
        na = NSEG // dil
        cq = blk // na
        qi = np.arange(blk)
        kj = np.arange(2 * blk)
        pos_q = na * (qi % cq) + qi // cq
        pos_k = na * (kj % (2 * cq)) + kj // (2 * cq)
        variants = []
        for q_off in (0, blk):
            dist = (pos_q[:, None] + q_off) - pos_k[None, :]
            variants.append(np.where((dist >= 0) & (dist <= blk), 0.0, MASK_VALUE))
        tables.append(np.stack(variants))
    return jnp.asarray(np.stack(tables).astype(np.float32))


def _attention(qkvg, seq):
    bsz, nseg, seg_len, _ = qkvg.shape
    n_pairs = D_ATTN // LANES

    def col_spec(which):
        return pl.BlockSpec((None, nseg, seg_len, LANES), lambda b, hp: (b, 0, 0, which * n_pairs + hp))

    seg_f32 = pltpu.VMEM((nseg, seg_len, LANES), F32)
    scratch = [seg_f32] * 3 + [pltpu.VMEM((2, nseg, seg_len, LANES), BF16), pltpu.VMEM((seq, LANES), F32)]
    scratch += [seg_f32] * 9
    return pl.pallas_call(
        _attn_kernel,
        grid=(bsz, n_pairs),
        in_specs=[col_spec(0), col_spec(1), col_spec(2), col_spec(3),
                  pl.BlockSpec((len(DILATED_PATTERNS), 2, ATTN_BLOCK, 2 * ATTN_BLOCK), lambda b, hp: (0, 0, 0, 0))],
        out_specs=pl.BlockSpec((None, seq, LANES), lambda b, hp: (b, 0, hp)),
        out_shape=jax.ShapeDtypeStruct((bsz, seq, D_ATTN), BF16),
        scratch_shapes=scratch,
        compiler_params=pltpu.CompilerParams(
            dimension_semantics=("parallel", "parallel"), vmem_limit_bytes=VMEM_LIMIT),
        name="dilated_attn",
    )(qkvg, qkvg, qkvg, qkvg, _attn_bias(seg_len))


SSD_T = 512
PREV_ROWS = 8


def _ssd_kernel(xbc_ref, prev_ref, z_ref, dt_ref, cw_ref, cb_ref, dtb_ref, alog_ref, dskip_ref,
                nw_ref, rexp_ref, tri_ref, y_ref, xpad_ref, state_ref, ydiag_ref):
    i = pl.program_id(1)

    @pl.when(i == 0)
    def _():
        state_ref[...] = jnp.zeros(state_ref.shape, F32)

    keep_prev = (i > 0).astype(F32)
    xpad_ref[0:PREV_ROWS, :] = prev_ref[...].astype(F32) * keep_prev
    xpad_ref[PREV_ROWS:, :] = xbc_ref[...].astype(F32)

    a_neg = -jnp.exp(alog_ref[...])
    li = lax.broadcasted_iota(jnp.int32, (CHUNK, CHUNK), 0)
    si = lax.broadcasted_iota(jnp.int32, (CHUNK, CHUNK), 1)
    causal = li >= si
    lane = lax.broadcasted_iota(jnp.int32, (CHUNK, LANES), 1)
    head0 = lane < SSM_HEAD_DIM
    tri = tri_ref[...]
    rexp = rexp_ref[...]

    def expand(e):
        hi = e.astype(BF16)
        lo = (e - hi.astype(F32)).astype(BF16)
        return (jnp.dot(hi, rexp, preferred_element_type=F32)
                + jnp.dot(lo, rexp, preferred_element_type=F32))

    for c in range(SSD_T // CHUNK):
        r0 = c * CHUNK
        conv = cb_ref[...]
        for kk in range(CONV_K):
            off = r0 + PREV_ROWS - (CONV_K - 1) + kk
            conv = conv + cw_ref[kk:kk + 1, :] * xpad_ref[off:off + CHUNK, :]
        xc = conv * _sigmoid(conv)
        xs = xc[:, :D_SSM]

        dt_in = dt_ref[r0:r0 + CHUNK, :] + dtb_ref[...]
        dt = jnp.maximum(dt_in, 0.0) + jnp.log1p(jnp.exp(-jnp.abs(dt_in)))
        a_dt = dt * a_neg
        a_cs = sum(jnp.dot(tri, part, preferred_element_type=F32) for part in _split3(a_dt))
        a_cs_t = a_cs.T
        a_last = a_cs[CHUNK - 1:CHUNK, :]
        exp_acs_x = expand(jnp.exp(a_cs))
        dt_x = expand(dt)
        dstate_x = expand(jnp.exp(a_last - a_cs))
        cdecay_x = exp_acs_x[CHUNK - 1:CHUNK, :]

        xdt = xs * dt_x
        xdt_b = xdt.astype(BF16)
        xdts_b = (xdt * dstate_x).astype(BF16)

        for g in range(SSM_GROUPS):
            gcols = slice(g * GROUP_W, (g + 1) * GROUP_W)
            b_f = xc[:, D_SSM + g * D_STATE:D_SSM + (g + 1) * D_STATE]
            c_b = xc[:, D_SSM + (SSM_GROUPS + g) * D_STATE:D_SSM + (SSM_GROUPS + g + 1) * D_STATE].astype(BF16)
            b_b = b_f.astype(BF16)
            cb = lax.dot_general(c_b, b_b, (((1,), (1,)), ((), ())), preferred_element_type=F32)
            s_prev = state_ref[g]
            y_off = jnp.dot(c_b, s_prev.astype(BF16), preferred_element_type=F32) * exp_acs_x[:, gcols]
            new_state = jnp.dot(b_f.T.astype(BF16), xdts_b[:, gcols], preferred_element_type=F32)
            state_ref[g] = s_prev * cdecay_x[:, gcols] + new_state
            for hp in range(HEADS_PER_GROUP // 2):
                pair = []
                pcols = slice(g * GROUP_W + hp * LANES, g * GROUP_W + (hp + 1) * LANES)
                for e in range(2):
                    h = g * HEADS_PER_GROUP + 2 * hp + e
                    seg = a_cs[:, h:h + 1] - a_cs_t[h:h + 1, :]
                    decay = jnp.exp(jnp.where(causal, seg, -jnp.inf))
                    pair.append(jnp.dot((cb * decay).astype(BF16), xdt_b[:, pcols], preferred_element_type=F32))
                ydiag_ref[:, pcols] = jnp.where(head0, pair[0], pair[1])
            y = ydiag_ref[:, gcols] + y_off + dskip_ref[:, gcols] * xs[:, gcols]
            zg = z_ref[r0:r0 + CHUNK, gcols].astype(F32)
            yz = y * (zg * _sigmoid(zg))
            ms = jnp.mean(yz * yz, axis=-1, keepdims=True)
            y_ref[r0:r0 + CHUNK, gcols] = (yz * lax.rsqrt(ms + NORM_EPS) * nw_ref[:, gcols]).astype(BF16)


def _ssd(xbc3, z3, dt3, conv_w, conv_b, dt_bias, a_log, d_skip, ssm_norm_w):
    bsz, seq, _ = xbc3.shape
    pad = LANES - SSM_HEADS
    dtb = jnp.pad(dt_bias, (0, pad))[None, :]
    alog = jnp.pad(a_log, (0, pad))[None, :]
    dskip = jnp.repeat(d_skip, SSM_HEAD_DIM)[None, :]
    rexp = (np.arange(LANES)[:, None] == (np.arange(D_SSM)[None, :] // SSM_HEAD_DIM))
    rexp = jnp.asarray(rexp.astype(np.float32), dtype=BF16)
    tri = jnp.asarray(np.tril(np.ones((CHUNK, CHUNK), np.float32)), dtype=BF16)
    steps_per_t = SSD_T // PREV_ROWS

    def const(shape):
        return pl.BlockSpec(shape, lambda b, i: (0,) * len(shape))

    return pl.pallas_call(
        _ssd_kernel,
        grid=(bsz, seq // SSD_T),
        in_specs=[
            pl.BlockSpec((None, SSD_T, D_CONV), lambda b, i: (b, i, 0)),
            pl.BlockSpec((None, PREV_ROWS, D_CONV), lambda b, i: (b, jnp.maximum(i * steps_per_t - 1, 0), 0)),
            pl.BlockSpec((None, SSD_T, D_SSM), lambda b, i: (b, i, 0)),
            pl.BlockSpec((None, SSD_T, LANES), lambda b, i: (b, i, 0)),
            const((CONV_K, D_CONV)), const((1, D_CONV)), const((1, LANES)), const((1, LANES)),
            const((1, D_SSM)), const((1, D_SSM)), const((LANES, D_SSM)), const((CHUNK, CHUNK)),
        ],
        out_specs=pl.BlockSpec((None, SSD_T, D_SSM), lambda b, i: (b, i, 0)),
        out_shape=jax.ShapeDtypeStruct((bsz, seq, D_SSM), BF16),
        scratch_shapes=[
            pltpu.VMEM((SSD_T + PREV_ROWS, D_CONV), F32),
            pltpu.VMEM((SSM_GROUPS, D_STATE, GROUP_W), F32),
            pltpu.VMEM((CHUNK, D_SSM), F32),
        ],
        compiler_params=pltpu.CompilerParams(
            dimension_semantics=("parallel", "arbitrary"), vmem_limit_bytes=VMEM_LIMIT),
        name="ssd",
    )(xbc3, xbc3, z3, dt3, conv_w, conv_b[None, :], dtb, alog, dskip, ssm_norm_w[None, :], rexp, tri)


OUT_TM = 512


def _outproj_kernel(attn_ref, y_ref, x_ref, w_ref, nw_ref, o_ref):
    out = jnp.dot(attn_ref[...], w_ref[:D_ATTN, :], preferred_element_type=F32)
    out = out + jnp.dot(y_ref[...], w_ref[D_ATTN:, :], preferred_element_type=F32)
    ms = jnp.mean(out * out, axis=-1, keepdims=True)
    o_ref[...] = x_ref[...] + out * lax.rsqrt(ms + NORM_EPS) * nw_ref[...]


def _outproj(attn2, y2, x2, w_out_b, norm_w):
    m = x2.shape[0]
    return pl.pallas_call(
        _outproj_kernel,
        grid=(m // OUT_TM,),
        in_specs=[
            pl.BlockSpec((OUT_TM, D_ATTN), lambda i: (i, 0)),
            pl.BlockSpec((OUT_TM, D_SSM), lambda i: (i, 0)),
            pl.BlockSpec((OUT_TM, D_MODEL), lambda i: (i, 0)),
            pl.BlockSpec((D_ATTN + D_SSM, D_MODEL), lambda i: (0, 0)),
            pl.BlockSpec((1, D_MODEL), lambda i: (0, 0)),
        ],
        out_specs=pl.BlockSpec((OUT_TM, D_MODEL), lambda i: (i, 0)),
        out_shape=jax.ShapeDtypeStruct((m, D_MODEL), F32),
        compiler_params=pltpu.CompilerParams(
            dimension_semantics=("parallel",), vmem_limit_bytes=VMEM_LIMIT),
        name="outproj",
    )(attn2, y2, x2, w_out_b, norm_w)


def _layer(hid, norm_pre_w, w_in, conv_w, conv_b, dt_bias, a_log, d_skip, ssm_norm_w, w_out, norm_post_w):
    bsz, seq, _ = hid.shape
    assert hid.shape[-1] == D_MODEL and w_in.shape == (D_MODEL, D_IN_PROJ)
    assert seq == 2 * ATTN_BLOCK * NSEG and seq % SSD_T == 0 and (bsz * seq) % OUT_TM == 0
    w_b = w_in.astype(BF16)
    w_ssm = jnp.pad(w_b[:, D_QKVG:], ((0, 0), (0, D_SSM_PAD - (D_IN_PROJ - D_QKVG))))
    qkvg, z, xbc, dt_raw = _inproj(hid, norm_pre_w[None, :], w_b[:, :D_QKVG], w_ssm)
    attn = _attention(qkvg, seq)
    y = _ssd(xbc, z, dt_raw, conv_w, conv_b, dt_bias, a_log, d_skip, ssm_norm_w)
    x2 = hid.reshape(bsz * seq, D_MODEL)
    out = _outproj(attn.reshape(bsz * seq, D_ATTN), y.reshape(bsz * seq, D_SSM), x2,
                   w_out.astype(BF16), norm_post_w[None, :])
    return out.reshape(bsz, seq, D_MODEL)


def kernel(x, norm_pre_w, w_in, conv_w, conv_b, dt_bias, a_log, d_skip, ssm_norm_w, w_out, norm_post_w):
    hid = x
    for layer in range(norm_pre_w.shape[0]):
        hid = _layer(hid, norm_pre_w[layer], w_in[layer], conv_w[layer], conv_b[layer],
                     dt_bias[layer], a_log[layer], d_skip[layer], ssm_norm_w[layer],
                     w_out[layer], norm_post_w[layer])
    return hid
```

```python
import functools

import jax
import jax.numpy as jnp
import numpy as np
from jax import lax
from jax.experimental import pallas as pl
from jax.experimental.pallas import tpu as pltpu

D_MODEL = 1024
ATTN_HEADS = 16
ATTN_HEAD_DIM = 64
D_ATTN = ATTN_HEADS * ATTN_HEAD_DIM
DILATED_PATTERNS = ((128, 1), (512, 4), (2048, 16))
ATTN_BLOCK = 128
D_SSM = 1024
SSM_HEAD_DIM = 64
SSM_HEADS = D_SSM // SSM_HEAD_DIM
SSM_GROUPS = 2
D_STATE = 128
CONV_K = 4
CHUNK = 128
D_CONV = D_SSM + 2 * SSM_GROUPS * D_STATE
D_QKVG = 4 * D_ATTN
D_IN_PROJ = D_QKVG + D_SSM + D_CONV + SSM_HEADS
NORM_EPS = 1e-6

LANES = 128
SUBLANES = 8
D_SSM_PAD = D_SSM + D_CONV + LANES
GROUP_W = D_SSM // SSM_GROUPS
HEADS_PER_GROUP = SSM_HEADS // SSM_GROUPS
MASK_VALUE = -1e30
VMEM_LIMIT = 56 * 1024 * 1024

NSEG = max(d for _, d in DILATED_PATTERNS)

F32 = jnp.float32
BF16 = jnp.bfloat16


def _silu(x):
    h = 0.5 * x
    return h + h * jnp.tanh(h)


def _split3(x):
    hi = x.astype(BF16)
    r1 = x - hi.astype(F32)
    mid = r1.astype(BF16)
    lo = (r1 - mid.astype(F32)).astype(BF16)
    return hi, mid, lo


IN_CHUNK = 512
IN_TM = 512
IN_ROWS = IN_TM // NSEG
IN_PREV = 16


def _inproj_kernel(x_ref, xprev_ref, nw_ref, wa_ref, ws_ref, cw_ref, cb_ref,
                   qkvg_ref, z_ref, xbc_ref, dt_ref, slab_ref, ur_ref, cx_ref):
    def normed(xf):
        ms = jnp.mean(xf * xf, axis=-1, keepdims=True)
        return xf * lax.rsqrt(ms + NORM_EPS) * nw_ref[...]

    uf = normed(x_ref[...])
    un = uf.astype(BF16)
    up = normed(xprev_ref[...]).astype(BF16)
    keep_prev = (pl.program_id(1) > 0).astype(F32)

    n_slabs = D_MODEL // LANES
    for c in range(n_slabs):
        slab_ref[c] = uf[:, c * LANES:(c + 1) * LANES]

    def proj(c0, width):
        return jnp.dot(un, ws_ref[:, c0:c0 + width], preferred_element_type=F32)

    def z_chunk(ci):
        c0 = ci * IN_CHUNK
        zc = proj(c0, IN_CHUNK)
        z_ref[:, c0:c0 + IN_CHUNK] = _silu(zc).astype(BF16)

    def conv_chunk(ci):
        c0 = ci * IN_CHUNK
        wcols = slice(D_SSM + c0, D_SSM + c0 + IN_CHUNK)
        cx_ref[ci, 0:IN_PREV, :] = jnp.dot(up, ws_ref[:, wcols], preferred_element_type=F32) * keep_prev
        cx_ref[ci, IN_PREV:, :] = proj(D_SSM + c0, IN_CHUNK)
        ext = cx_ref[ci]
        conv = cb_ref[:, c0:c0 + IN_CHUNK] + cw_ref[CONV_K - 1:CONV_K, c0:c0 + IN_CHUNK] * ext
        for back in range(1, CONV_K):
            tap = cw_ref[CONV_K - 1 - back:CONV_K - back, c0:c0 + IN_CHUNK]
            conv = conv + tap * pltpu.roll(ext, back, axis=0)
        conv = conv[IN_PREV:, :]
        xbc_ref[:, c0:c0 + IN_CHUNK] = _silu(conv).astype(BF16)

    def regroup():
        for c in range(n_slabs):
            for r in range(NSEG):
                ur_ref[r * IN_ROWS:(r + 1) * IN_ROWS, c * LANES:(c + 1) * LANES] = (
                    slab_ref[c, pl.ds(r, IN_ROWS, stride=NSEG), :].astype(BF16))

    def attn_chunk(ci):
        c0 = ci * IN_CHUNK
        res = jnp.dot(ur_ref[...], wa_ref[:, c0:c0 + IN_CHUNK], preferred_element_type=F32)
        if c0 >= 3 * D_ATTN:
            res = _silu(res)
        res = res.astype(BF16)
        for r in range(NSEG):
            qkvg_ref[r, :, c0:c0 + IN_CHUNK] = res[r * IN_ROWS:(r + 1) * IN_ROWS, :]

    n_attn = D_QKVG // IN_CHUNK
    n_gate = D_ATTN // IN_CHUNK
    heavy = ([(conv_chunk, ci) for ci in range(D_CONV // IN_CHUNK)] + [(z_chunk, ci) for ci in range(D_SSM // IN_CHUNK)]
             + [(attn_chunk, ci) for ci in range(n_attn - n_gate, n_attn)])
    plain = [(attn_chunk, ci) for ci in range(n_attn - n_gate)]
    heavy[0][0](heavy[0][1])
    regroup()
    rest = heavy[1:]
    while rest or plain:
        if plain:
            fn, ci = plain.pop(0)
            fn(ci)
        if rest:
            fn, ci = rest.pop(0)
            fn(ci)
    dt_ref[...] = proj(D_SSM + D_CONV, LANES)


def _inproj(x, norm_w, w_attn, w_ssm, conv_w, conv_b):
    bsz, seq, _ = x.shape
    seg_len = seq // NSEG
    prev_per_tile = IN_TM // IN_PREV
    return pl.pallas_call(
        _inproj_kernel,
        grid=(bsz, seq // IN_TM),
        in_specs=[
            pl.BlockSpec((None, IN_TM, D_MODEL), lambda b, k: (b, k, 0)),
            pl.BlockSpec((None, IN_PREV, D_MODEL), lambda b, k: (b, jnp.maximum(k * prev_per_tile - 1, 0), 0)),
            pl.BlockSpec((1, D_MODEL), lambda b, k: (0, 0)),
            pl.BlockSpec((D_MODEL, D_QKVG), lambda b, k: (0, 0), pipeline_mode=pl.Buffered(1)),
            pl.BlockSpec((D_MODEL, D_SSM_PAD), lambda b, k: (0, 0), pipeline_mode=pl.Buffered(1)),
            pl.BlockSpec((CONV_K, D_CONV), lambda b, k: (0, 0)),
            pl.BlockSpec((1, D_CONV), lambda b, k: (0, 0)),
        ],
        out_specs=[
            pl.BlockSpec((None, NSEG, IN_ROWS, D_QKVG), lambda b, k: (b, 0, k, 0)),
            pl.BlockSpec((None, IN_TM, D_SSM), lambda b, k: (b, k, 0)),
            pl.BlockSpec((None, IN_TM, D_CONV), lambda b, k: (b, k, 0)),
            pl.BlockSpec((None, IN_TM, LANES), lambda b, k: (b, k, 0)),
        ],
        out_shape=[
            jax.ShapeDtypeStruct((bsz, NSEG, seg_len, D_QKVG), BF16),
            jax.ShapeDtypeStruct((bsz, seq, D_SSM), BF16),
            jax.ShapeDtypeStruct((bsz, seq, D_CONV), BF16),
            jax.ShapeDtypeStruct((bsz, seq, LANES), F32),
        ],
        scratch_shapes=[
            pltpu.VMEM((D_MODEL // LANES, IN_TM, LANES), F32),
            pltpu.VMEM((IN_TM, D_MODEL), BF16),
            pltpu.VMEM((D_CONV // IN_CHUNK, IN_PREV + IN_TM, IN_CHUNK), F32),
        ],
        compiler_params=pltpu.CompilerParams(
            dimension_semantics=("parallel", "parallel"), vmem_limit_bytes=VMEM_LIMIT),
        name="inproj",
    )(x, x, norm_w, w_attn, w_ssm, conv_w, conv_b)


P1_UNROLL = 8
P2_UNROLL = 8
P3_UNROLL = 4
OUT_ROWS = 512


def _attn_kernel(q_ref, k_ref, v_ref, g_ref, bias_ref, o_ref,
                 qf_ref, kf_ref, vf_ref, qm_ref, onat_ref, *stat_refs):
    acc_refs, m_refs, l_refs = stat_refs[0:3], stat_refs[3:6], stat_refs[6:9]
    nseg, seg_len, _ = q_ref.shape
    blk = ATTN_BLOCK
    head0 = lax.broadcasted_iota(jnp.int32, (blk, LANES), 1) < ATTN_HEAD_DIM
    head0_seg = lax.broadcasted_iota(jnp.int32, (seg_len, LANES), 1) < ATTN_HEAD_DIM

    def prep(r, carry):
        qf = q_ref[r].astype(F32)
        qf_ref[r] = qf
        qm_ref[0, r] = jnp.where(head0_seg, qf, 0.0).astype(BF16)
        qm_ref[1, r] = jnp.where(head0_seg, 0.0, qf).astype(BF16)
        kf_ref[r] = k_ref[r].astype(F32)
        vf_ref[r] = v_ref[r].astype(F32)
        return carry

    lax.fori_loop(0, nseg, prep, 0)

    def attend(q01, kw, vw, bias):
        nq, nk = bias.shape
        s = lax.dot_general(q01, kw, (((1,), (1,)), ((), ())), preferred_element_type=F32)
        s = s + jnp.concatenate([bias, bias], axis=0)
        m = jnp.max(s, axis=-1, keepdims=True)
        pr = jnp.exp(s - m).astype(BF16)
        vw1 = jnp.concatenate([vw, jnp.ones((nk, LANES), BF16)], axis=1)
        oa = jnp.dot(pr, vw1, preferred_element_type=F32)
        mb = jnp.broadcast_to(m, (2 * nq, LANES))
        outs = []
        for i0 in range(0, nq, blk):
            lo, hi = slice(i0, i0 + blk), slice(nq + i0, nq + i0 + blk)
            outs.append((jnp.where(head0, oa[lo, :LANES], oa[hi, :LANES]),
                         jnp.where(head0, mb[lo], mb[hi]),
                         jnp.where(head0, oa[lo, LANES:], oa[hi, LANES:])))
        return outs

    def p1_body(j, carry):
        iq = pl.multiple_of(j * SUBLANES, SUBLANES)
        ik = pl.multiple_of(jnp.maximum(j - 1, 0) * SUBLANES, SUBLANES)
        qf = jnp.concatenate([qf_ref[r, pl.ds(iq, SUBLANES), :] for r in range(nseg)], axis=0)
        q01 = jnp.concatenate([jnp.where(head0, qf, 0.0), jnp.where(head0, 0.0, qf)], axis=0).astype(BF16)
        kw = jnp.concatenate([kf_ref[r, pl.ds(ik, 2 * SUBLANES), :] for r in range(nseg)], axis=0).astype(BF16)
        vw = jnp.concatenate([vf_ref[r, pl.ds(ik, 2 * SUBLANES), :] for r in range(nseg)], axis=0).astype(BF16)
        (acc, mx, den), = attend(q01, kw, vw, bias_ref[0, jnp.minimum(j, 1)])
        for r in range(nseg):
            rows = slice(r * SUBLANES, (r + 1) * SUBLANES)
            acc_refs[0][r, pl.ds(iq, SUBLANES), :] = acc[rows]
            m_refs[0][r, pl.ds(iq, SUBLANES), :] = mx[rows]
            l_refs[0][r, pl.ds(iq, SUBLANES), :] = den[rows]
        return carry

    lax.fori_loop(0, seg_len // SUBLANES, p1_body, 0, unroll=P1_UNROLL)

    dil2 = DILATED_PATTERNS[1][1]
    na = nseg // dil2
    cq = blk // na
    nb2 = seg_len // cq

    def p2_body(t, carry):
        r4 = t // nb2
        j = t % nb2
        iq = pl.multiple_of(j * cq, cq)
        ik = pl.multiple_of(jnp.maximum(j - 1, 0) * cq, cq)
        segs = [dil2 * a + r4 for a in range(na)]
        q01 = jnp.concatenate([qm_ref[h, sg, pl.ds(iq, cq), :] for h in range(2) for sg in segs], axis=0)
        kw = jnp.concatenate([k_ref[sg, pl.ds(ik, 2 * cq), :] for sg in segs], axis=0)
        vw = jnp.concatenate([v_ref[sg, pl.ds(ik, 2 * cq), :] for sg in segs], axis=0)
        (acc, mx, den), = attend(q01, kw, vw, bias_ref[1, jnp.minimum(j, 1)])
        for a, sg in enumerate(segs):
            rows = slice(a * cq, (a + 1) * cq)
            acc_refs[1][sg, pl.ds(iq, cq), :] = acc[rows]
            m_refs[1][sg, pl.ds(iq, cq), :] = mx[rows]
            l_refs[1][sg, pl.ds(iq, cq), :] = den[rows]
        return carry

    lax.fori_loop(0, dil2 * nb2, p2_body, 0, unroll=P2_UNROLL)

    assert seg_len == 2 * blk and DILATED_PATTERNS[2][1] == nseg

    def p3_body(r, carry):
        q01 = jnp.concatenate([qm_ref[0, r], qm_ref[1, r]], axis=0)
        bias = jnp.concatenate([bias_ref[2, 0], bias_ref[2, 1]], axis=0)
        outs = attend(q01, k_ref[r], v_ref[r], bias)
        for jb, (acc, mx, den) in enumerate(outs):
            rows = slice(jb * blk, (jb + 1) * blk)
            acc_refs[2][r, rows, :] = acc
            m_refs[2][r, rows, :] = mx
            l_refs[2][r, rows, :] = den
        return carry

    lax.fori_loop(0, nseg, p3_body, 0, unroll=P3_UNROLL)

    def merge_body(r, carry):
        ms = [m_refs[p][r] for p in range(3)]
        mtop = jnp.maximum(jnp.maximum(ms[0], ms[1]), ms[2])
        num = jnp.zeros((seg_len, LANES), F32)
        den = jnp.zeros((seg_len, LANES), F32)
        for p in range(3):
            e = jnp.exp(ms[p] - mtop)
            num = num + e * acc_refs[p][r]
            den = den + e * l_refs[p][r]
        onat_ref[pl.ds(r, seg_len, stride=nseg), :] = num / den * g_ref[r].astype(F32)
        return carry

    lax.fori_loop(0, nseg, merge_body, 0)

    def out_body(i, carry):
        rows = pl.ds(pl.multiple_of(i * OUT_ROWS, OUT_ROWS), OUT_ROWS)
        o_ref[rows, :] = onat_ref[rows, :].astype(BF16)
        return carry

    lax.fori_loop(0, nseg * seg_len // OUT_ROWS, out_body, 0)


def _attn_bias(seg_len):
    blk = ATTN_BLOCK
    tables = []
    for _, dil in DILATED_PATTERNS:
        na = NSEG // dil
        cq = blk // na
        qi = np.arange(blk)
        kj = np.arange(2 * blk)
        pos_q = na * (qi % cq) + qi // cq
        pos_k = na * (kj % (2 * cq)) + kj // (2 * cq)
        variants = []
        for q_off in (0, blk):
            dist = (pos_q[:, None] + q_off) - pos_k[None, :]
            variants.append(np.where((dist >= 0) & (dist <= blk), 0.0, MASK_VALUE))
        tables.append(np.stack(variants))
    return jnp.asarray(np.stack(tables).astype(np.float32))


def _attention(qkvg, seq):
    bsz, nseg, seg_len, _ = qkvg.shape
    n_pairs = D_ATTN // LANES

    def col_spec(which):
        return pl.BlockSpec((None, nseg, seg_len, LANES), lambda b, hp: (b, 0, 0, which * n_pairs + hp))

    seg_f32 = pltpu.VMEM((nseg, seg_len, LANES), F32)
    scratch = [seg_f32] * 3 + [pltpu.VMEM((2, nseg, seg_len, LANES), BF16), pltpu.VMEM((seq, LANES), F32)]
    scratch += [seg_f32] * 9
    return pl.pallas_call(
        _attn_kernel,
        grid=(bsz, n_pairs),
        in_specs=[col_spec(0), col_spec(1), col_spec(2), col_spec(3),
                  pl.BlockSpec((len(DILATED_PATTERNS), 2, ATTN_BLOCK, 2 * ATTN_BLOCK), lambda b, hp: (0, 0, 0, 0))],
        out_specs=pl.BlockSpec((None, seq, LANES), lambda b, hp: (b, 0, hp)),
        out_shape=jax.ShapeDtypeStruct((bsz, seq, D_ATTN), BF16),
        scratch_shapes=scratch,
        compiler_params=pltpu.CompilerParams(
            dimension_semantics=("parallel", "parallel"), vmem_limit_bytes=VMEM_LIMIT),
        name="dilated_attn",
    )(qkvg, qkvg, qkvg, qkvg, _attn_bias(seg_len))


SSD_T = 512


def _ssd_kernel(xbc_ref, z_ref, dt_ref, dtb_ref, alog_ref, dskip_ref,
                nw_ref, rexp_ref, tri_ref, y_ref, state_ref, ydiag_ref):
    @pl.when(pl.program_id(1) == 0)
    def _():
        state_ref[...] = jnp.zeros(state_ref.shape, F32)

    a_neg = -jnp.exp(alog_ref[...])
    li = lax.broadcasted_iota(jnp.int32, (CHUNK, CHUNK), 0)
    si = lax.broadcasted_iota(jnp.int32, (CHUNK, CHUNK), 1)
    causal = li >= si
    lane = lax.broadcasted_iota(jnp.int32, (CHUNK, LANES), 1)
    head0 = lane < SSM_HEAD_DIM
    tri = tri_ref[...]
    rexp = rexp_ref[...]

    def expand(e):
        hi = e.astype(BF16)
        lo = (e - hi.astype(F32)).astype(BF16)
        return (jnp.dot(hi, rexp, preferred_element_type=F32)
                + jnp.dot(lo, rexp, preferred_element_type=F32))

    for c in range(SSD_T // CHUNK):
        r0 = c * CHUNK
        xs = xbc_ref[r0:r0 + CHUNK, :D_SSM].astype(F32)

        dt_in = dt_ref[r0:r0 + CHUNK, :] + dtb_ref[...]
        dt = jnp.maximum(dt_in, 0.0) + jnp.log1p(jnp.exp(-jnp.abs(dt_in)))
        a_dt = dt * a_neg
        a_cs = sum(jnp.dot(tri, part, preferred_element_type=F32) for part in _split3(a_dt))
        a_cs_t = a_cs.T
        a_last = a_cs[CHUNK - 1:CHUNK, :]
        exp_acs_x = expand(jnp.exp(a_cs))
        dt_x = expand(dt)
        dstate_x = expand(jnp.exp(a_last - a_cs))
        cdecay_x = exp_acs_x[CHUNK - 1:CHUNK, :]

        xdt = xs * dt_x
        xdt_b = xdt.astype(BF16)
        xdts_b = (xdt * dstate_x).astype(BF16)

        for g in range(SSM_GROUPS):
            gcols = slice(g * GROUP_W, (g + 1) * GROUP_W)
            b_b = xbc_ref[r0:r0 + CHUNK, D_SSM + g * D_STATE:D_SSM + (g + 1) * D_STATE]
            c_b = xbc_ref[r0:r0 + CHUNK, D_SSM + (SSM_GROUPS + g) * D_STATE:D_SSM + (SSM_GROUPS + g + 1) * D_STATE]
            b_f = b_b.astype(F32)
            cb = lax.dot_general(c_b, b_b, (((1,), (1,)), ((), ())), preferred_element_type=F32)
            s_prev = state_ref[g]
            y_off = jnp.dot(c_b, s_prev.astype(BF16), preferred_element_type=F32) * exp_acs_x[:, gcols]
            new_state = jnp.dot(b_f.T.astype(BF16), xdts_b[:, gcols], preferred_element_type=F32)
            state_ref[g] = s_prev * cdecay_x[:, gcols] + new_state
            for hp in range(HEADS_PER_GROUP // 2):
                pair = []
                pcols = slice(g * GROUP_W + hp * LANES, g * GROUP_W + (hp + 1) * LANES)
                for e in range(2):
                    h = g * HEADS_PER_GROUP + 2 * hp + e
                    seg = a_cs[:, h:h + 1] - a_cs_t[h:h + 1, :]
                    decay = jnp.exp(jnp.where(causal, seg, -jnp.inf))
                    pair.append(jnp.dot((cb * decay).astype(BF16), xdt_b[:, pcols], preferred_element_type=F32))
                ydiag_ref[:, pcols] = jnp.where(head0, pair[0], pair[1])
            y = ydiag_ref[:, gcols] + y_off + dskip_ref[:, gcols] * xs[:, gcols]
            yz = y * z_ref[r0:r0 + CHUNK, gcols].astype(F32)
            ms = jnp.mean(yz * yz, axis=-1, keepdims=True)
            y_ref[r0:r0 + CHUNK, gcols] = (yz * lax.rsqrt(ms + NORM_EPS) * nw_ref[:, gcols]).astype(BF16)


def _ssd(xbc3, z3, dt3, dt_bias, a_log, d_skip, ssm_norm_w):
    bsz, seq, _ = xbc3.shape
    pad = LANES - SSM_HEADS
    dtb = jnp.pad(dt_bias, (0, pad))[None, :]
    alog = jnp.pad(a_log, (0, pad))[None, :]
    dskip = jnp.repeat(d_skip, SSM_HEAD_DIM)[None, :]
    rexp = (np.arange(LANES)[:, None] == (np.arange(D_SSM)[None, :] // SSM_HEAD_DIM))
    rexp = jnp.asarray(rexp.astype(np.float32), dtype=BF16)
    tri = jnp.asarray(np.tril(np.ones((CHUNK, CHUNK), np.float32)), dtype=BF16)

    def const(shape):
        return pl.BlockSpec(shape, lambda b, i: (0,) * len(shape))

    return pl.pallas_call(
        _ssd_kernel,
        grid=(bsz, seq // SSD_T),
        in_specs=[
            pl.BlockSpec((None, SSD_T, D_CONV), lambda b, i: (b, i, 0)),
            pl.BlockSpec((None, SSD_T, D_SSM), lambda b, i: (b, i, 0)),
            pl.BlockSpec((None, SSD_T, LANES), lambda b, i: (b, i, 0)),
            const((1, LANES)), const((1, LANES)),
            const((1, D_SSM)), const((1, D_SSM)), const((LANES, D_SSM)), const((CHUNK, CHUNK)),
        ],
        out_specs=pl.BlockSpec((None, SSD_T, D_SSM), lambda b, i: (b, i, 0)),
        out_shape=jax.ShapeDtypeStruct((bsz, seq, D_SSM), BF16),
        scratch_shapes=[
            pltpu.VMEM((SSM_GROUPS, D_STATE, GROUP_W), F32),
            pltpu.VMEM((CHUNK, D_SSM), F32),
        ],
        compiler_params=pltpu.CompilerParams(
            dimension_semantics=("parallel", "arbitrary"), vmem_limit_bytes=VMEM_LIMIT),
        name="ssd",
    )(xbc3, z3, dt3, dtb, alog, dskip, ssm_norm_w[None, :], rexp, tri)


OUT_TM = 512


def _outproj_kernel(attn_ref, y_ref, x_ref, w_ref, nw_ref, o_ref):
    out = jnp.dot(attn_ref[...], w_ref[:D_ATTN, :], preferred_element_type=F32)
    out = out + jnp.dot(y_ref[...], w_ref[D_ATTN:, :], preferred_element_type=F32)
    ms = jnp.mean(out * out, axis=-1, keepdims=True)
    o_ref[...] = x_ref[...] + out * lax.rsqrt(ms + NORM_EPS) * nw_ref[...]


def _outproj(attn2, y2, x2, w_out_b, norm_w):
    m = x2.shape[0]
    return pl.pallas_call(
        _outproj_kernel,
        grid=(m // OUT_TM,),
        in_specs=[
            pl.BlockSpec((OUT_TM, D_ATTN), lambda i: (i, 0)),
            pl.BlockSpec((OUT_TM, D_SSM), lambda i: (i, 0)),
            pl.BlockSpec((OUT_TM, D_MODEL), lambda i: (i, 0)),
            pl.BlockSpec((D_ATTN + D_SSM, D_MODEL), lambda i: (0, 0)),
            pl.BlockSpec((1, D_MODEL), lambda i: (0, 0)),
        ],
        out_specs=pl.BlockSpec((OUT_TM, D_MODEL), lambda i: (i, 0)),
        out_shape=jax.ShapeDtypeStruct((m, D_MODEL), F32),
        compiler_params=pltpu.CompilerParams(
            dimension_semantics=("parallel",), vmem_limit_bytes=VMEM_LIMIT),
        name="outproj",
    )(attn2, y2, x2, w_out_b, norm_w)


def _layer(hid, norm_pre_w, w_in, conv_w, conv_b, dt_bias, a_log, d_skip, ssm_norm_w, w_out, norm_post_w):
    bsz, seq, _ = hid.shape
    assert hid.shape[-1] == D_MODEL and w_in.shape == (D_MODEL, D_IN_PROJ)
    assert seq == 2 * ATTN_BLOCK * NSEG and seq % SSD_T == 0 and (bsz * seq) % OUT_TM == 0
    w_b = w_in.astype(BF16)
    w_ssm = jnp.pad(w_b[:, D_QKVG:], ((0, 0), (0, D_SSM_PAD - (D_IN_PROJ - D_QKVG))))
    q_scale = ATTN_HEAD_DIM ** -0.5
    assert np.log2(q_scale) == round(np.log2(q_scale))
    w_attn = jnp.concatenate([w_b[:, :D_ATTN] * q_scale, w_b[:, D_ATTN:D_QKVG]], axis=1)
    qkvg, z, xbc, dt_raw = _inproj(hid, norm_pre_w[None, :], w_attn, w_ssm, conv_w, conv_b[None, :])
    attn = _attention(qkvg, seq)
    y = _ssd(xbc, z, dt_raw, dt_bias, a_log, d_skip, ssm_norm_w)
    x2 = hid.reshape(bsz * seq, D_MODEL)
    out = _outproj(attn.reshape(bsz * seq, D_ATTN), y.reshape(bsz * seq, D_SSM), x2,
                   w_out.astype(BF16), norm_post_w[None, :])
    return out.reshape(bsz, seq, D_MODEL)


def kernel(x, norm_pre_w, w_in, conv_w, conv_b, dt_bias, a_log, d_skip, ssm_norm_w, w_out, norm_post_w):
    hid = x
    for layer in range(norm_pre_w.shape[0]):
        hid = _layer(hid, norm_pre_w[layer], w_in[layer], conv_w[layer], conv_b[layer],
                     dt_bias[layer], a_log[layer], d_skip[layer], ssm_norm_w[layer],
                     w_out[layer], norm_post_w[layer])
    return hid
```

```python
import functools

import jax
import jax.numpy as jnp
import numpy as np
from jax import lax
from jax.experimental import pallas as pl
from jax.experimental.pallas import tpu as pltpu

D_MODEL = 1024
ATTN_HEADS = 16
ATTN_HEAD_DIM = 64
D_ATTN = ATTN_HEADS * ATTN_HEAD_DIM
DILATED_PATTERNS = ((128, 1), (512, 4), (2048, 16))
ATTN_BLOCK = 128
D_SSM = 1024
SSM_HEAD_DIM = 64
SSM_HEADS = D_SSM // SSM_HEAD_DIM
SSM_GROUPS = 2
D_STATE = 128
CONV_K = 4
CHUNK = 128
D_CONV = D_SSM + 2 * SSM_GROUPS * D_STATE
D_QKVG = 4 * D_ATTN
D_IN_PROJ = D_QKVG + D_SSM + D_CONV + SSM_HEADS
NORM_EPS = 1e-6

LANES = 128
SUBLANES = 8
D_SSM_PAD = D_SSM + D_CONV + LANES
GROUP_W = D_SSM // SSM_GROUPS
HEADS_PER_GROUP = SSM_HEADS // SSM_GROUPS
MASK_VALUE = -1e30
VMEM_LIMIT = 56 * 1024 * 1024

NSEG = max(d for _, d in DILATED_PATTERNS)

F32 = jnp.float32
BF16 = jnp.bfloat16


def _silu(x):
    h = 0.5 * x
    return h + h * jnp.tanh(h)


def _split3(x):
    hi = x.astype(BF16)
    r1 = x - hi.astype(F32)
    mid = r1.astype(BF16)
    lo = (r1 - mid.astype(F32)).astype(BF16)
    return hi, mid, lo


IN_CHUNK = 512
IN_TM = 512
IN_ROWS = IN_TM // NSEG
IN_PREV = 16


def _inproj_kernel(x_ref, xprev_ref, nw_ref, wa_ref, ws_ref, cw_ref, cb_ref,
                   qkvg_ref, z_ref, xbc_ref, dt_ref, slab_ref, ur_ref, cx_ref):
    def normed(xf):
        ms = jnp.mean(xf * xf, axis=-1, keepdims=True)
        return xf * lax.rsqrt(ms + NORM_EPS) * nw_ref[...]

    uf = normed(x_ref[...])
    un = uf.astype(BF16)
    up = normed(xprev_ref[...]).astype(BF16)
    keep_prev = (pl.program_id(1) > 0).astype(F32)

    n_slabs = D_MODEL // LANES
    for c in range(n_slabs):
        slab_ref[c] = uf[:, c * LANES:(c + 1) * LANES]

    def proj(c0, width):
        return jnp.dot(un, ws_ref[:, c0:c0 + width], preferred_element_type=F32)

    def z_chunk(ci):
        c0 = ci * IN_CHUNK
        zc = proj(c0, IN_CHUNK)
        z_ref[:, c0:c0 + IN_CHUNK] = _silu(zc).astype(BF16)

    def conv_chunk(ci):
        c0 = ci * IN_CHUNK
        wcols = slice(D_SSM + c0, D_SSM + c0 + IN_CHUNK)
        cx_ref[ci, 0:IN_PREV, :] = jnp.dot(up, ws_ref[:, wcols], preferred_element_type=F32) * keep_prev
        cx_ref[ci, IN_PREV:, :] = proj(D_SSM + c0, IN_CHUNK)
        ext = cx_ref[ci]
        conv = cb_ref[:, c0:c0 + IN_CHUNK] + cw_ref[CONV_K - 1:CONV_K, c0:c0 + IN_CHUNK] * ext
        for back in range(1, CONV_K):
            tap = cw_ref[CONV_K - 1 - back:CONV_K - back, c0:c0 + IN_CHUNK]
            conv = conv + tap * pltpu.roll(ext, back, axis=0)
        conv = conv[IN_PREV:, :]
        xbc_ref[:, c0:c0 + IN_CHUNK] = _silu(conv).astype(BF16)

    def regroup():
        for c in range(n_slabs):
            for r in range(NSEG):
                ur_ref[r * IN_ROWS:(r + 1) * IN_ROWS, c * LANES:(c + 1) * LANES] = (
                    slab_ref[c, pl.ds(r, IN_ROWS, stride=NSEG), :].astype(BF16))

    def attn_chunk(ci):
        c0 = ci * IN_CHUNK
        res = jnp.dot(ur_ref[...], wa_ref[:, c0:c0 + IN_CHUNK], preferred_element_type=F32)
        if c0 >= 3 * D_ATTN:
            res = _silu(res)
        res = res.astype(BF16)
        for r in range(NSEG):
            qkvg_ref[r, :, c0:c0 + IN_CHUNK] = res[r * IN_ROWS:(r + 1) * IN_ROWS, :]

    n_attn = D_QKVG // IN_CHUNK
    n_gate = D_ATTN // IN_CHUNK
    heavy = ([(conv_chunk, ci) for ci in range(D_CONV // IN_CHUNK)] + [(z_chunk, ci) for ci in range(D_SSM // IN_CHUNK)]
             + [(attn_chunk, ci) for ci in range(n_attn - n_gate, n_attn)])
    plain = [(attn_chunk, ci) for ci in range(n_attn - n_gate)]
    heavy[0][0](heavy[0][1])
    regroup()
    rest = heavy[1:]
    while rest or plain:
        if plain:
            fn, ci = plain.pop(0)
            fn(ci)
        if rest:
            fn, ci = rest.pop(0)
            fn(ci)
    dt_ref[...] = proj(D_SSM + D_CONV, LANES)


def _inproj(x, norm_w, w_attn, w_ssm, conv_w, conv_b):
    bsz, seq, _ = x.shape
    seg_len = seq // NSEG
    prev_per_tile = IN_TM // IN_PREV
    return pl.pallas_call(
        _inproj_kernel,
        grid=(bsz, seq // IN_TM),
        in_specs=[
            pl.BlockSpec((None, IN_TM, D_MODEL), lambda b, k: (b, k, 0)),
            pl.BlockSpec((None, IN_PREV, D_MODEL), lambda b, k: (b, jnp.maximum(k * prev_per_tile - 1, 0), 0)),
            pl.BlockSpec((1, D_MODEL), lambda b, k: (0, 0)),
            pl.BlockSpec((D_MODEL, D_QKVG), lambda b, k: (0, 0), pipeline_mode=pl.Buffered(1)),
            pl.BlockSpec((D_MODEL, D_SSM_PAD), lambda b, k: (0, 0), pipeline_mode=pl.Buffered(1)),
            pl.BlockSpec((CONV_K, D_CONV), lambda b, k: (0, 0)),
            pl.BlockSpec((1, D_CONV), lambda b, k: (0, 0)),
        ],
        out_specs=[
            pl.BlockSpec((None, NSEG, IN_ROWS, D_QKVG), lambda b, k: (b, 0, k, 0)),
            pl.BlockSpec((None, IN_TM, D_SSM), lambda b, k: (b, k, 0)),
            pl.BlockSpec((None, IN_TM, D_CONV), lambda b, k: (b, k, 0)),
            pl.BlockSpec((None, IN_TM, LANES), lambda b, k: (b, k, 0)),
        ],
        out_shape=[
            jax.ShapeDtypeStruct((bsz, NSEG, seg_len, D_QKVG), BF16),
            jax.ShapeDtypeStruct((bsz, seq, D_SSM), BF16),
            jax.ShapeDtypeStruct((bsz, seq, D_CONV), BF16),
            jax.ShapeDtypeStruct((bsz, seq, LANES), F32),
        ],
        scratch_shapes=[
            pltpu.VMEM((D_MODEL // LANES, IN_TM, LANES), F32),
            pltpu.VMEM((IN_TM, D_MODEL), BF16),
            pltpu.VMEM((D_CONV // IN_CHUNK, IN_PREV + IN_TM, IN_CHUNK), F32),
        ],
        compiler_params=pltpu.CompilerParams(
            dimension_semantics=("parallel", "parallel"), vmem_limit_bytes=VMEM_LIMIT),
        name="inproj",
    )(x, x, norm_w, w_attn, w_ssm, conv_w, conv_b)


P1_UNROLL = 16
P2_UNROLL = 16
P3_UNROLL = 16
OUT_ROWS = 512


def _attn_kernel(q_ref, k_ref, v_ref, g_ref, eye_ref, biast_ref, o_ref,
                 qf_ref, kf_ref, vf_ref, qm_ref, vm_ref, onat_ref, *stat_refs):
    acc_refs, m_refs, l_refs = stat_refs[0:3], stat_refs[3:6], stat_refs[6:9]
    nseg, seg_len, _ = q_ref.shape
    blk = ATTN_BLOCK
    head0 = lax.broadcasted_iota(jnp.int32, (blk, LANES), 1) < ATTN_HEAD_DIM
    head0_seg = lax.broadcasted_iota(jnp.int32, (seg_len, LANES), 1) < ATTN_HEAD_DIM
    head0_win = lax.broadcasted_iota(jnp.int32, (2 * blk, LANES), 1) < ATTN_HEAD_DIM
    ones_h = (jnp.where(head0_win, 1.0, 0.0).astype(BF16), jnp.where(head0_win, 0.0, 1.0).astype(BF16))

    def prep(r, carry):
        qf = q_ref[r].astype(F32)
        vf = v_ref[r].astype(F32)
        for h in range(2):
            keep = head0_seg if h == 0 else jnp.logical_not(head0_seg)
            qh = jnp.where(keep, qf, 0.0)
            vh = jnp.where(keep, vf, 0.0)
            qf_ref[h, r] = qh
            vf_ref[h, r] = vh
            qm_ref[h, r] = qh.astype(BF16)
            vm_ref[h, r] = vh.astype(BF16)
        kf_ref[r] = k_ref[r].astype(F32)
        return carry

    lax.fori_loop(0, nseg, prep, 0)

    def attend(q01, kw, v0, v1, bias_t):
        qa = jnp.concatenate([q01, eye_ref[...]], axis=1)
        ka = jnp.concatenate([kw, bias_t], axis=1)
        s = lax.dot_general(qa, ka, (((1,), (1,)), ((), ())), preferred_element_type=F32)
        m = jnp.max(s, axis=-1, keepdims=True)
        pr = jnp.exp2(s - m).astype(BF16)
        lhs = jnp.concatenate([pr[:blk], pr[blk:]], axis=1)
        rhs = jnp.concatenate([jnp.concatenate([v0, ones_h[0]], axis=1),
                               jnp.concatenate([v1, ones_h[1]], axis=1)], axis=0)
        oa = jnp.dot(lhs, rhs, preferred_element_type=F32)
        mb = jnp.broadcast_to(m, (2 * blk, LANES))
        return oa[:, :LANES], jnp.where(head0, mb[:blk], mb[blk:]), oa[:, LANES:]

    def p1_body(j, carry):
        iq = pl.multiple_of(j * SUBLANES, SUBLANES)
        ik = pl.multiple_of(jnp.maximum(j - 1, 0) * SUBLANES, SUBLANES)
        q01 = jnp.concatenate([qf_ref[h, r, pl.ds(iq, SUBLANES), :] for h in range(2) for r in range(nseg)],
                              axis=0).astype(BF16)
        kw = jnp.concatenate([kf_ref[r, pl.ds(ik, 2 * SUBLANES), :] for r in range(nseg)], axis=0).astype(BF16)
        v0, v1 = [jnp.concatenate([vf_ref[h, r, pl.ds(ik, 2 * SUBLANES), :] for r in range(nseg)],
                                  axis=0).astype(BF16) for h in range(2)]
        acc, mx, den = attend(q01, kw, v0, v1, biast_ref[0, jnp.minimum(j, 1)])
        for r in range(nseg):
            rows = slice(r * SUBLANES, (r + 1) * SUBLANES)
            acc_refs[0][r, pl.ds(iq, SUBLANES), :] = acc[rows]
            m_refs[0][r, pl.ds(iq, SUBLANES), :] = mx[rows]
            l_refs[0][r, pl.ds(iq, SUBLANES), :] = den[rows]
        return carry

    lax.fori_loop(0, seg_len // SUBLANES, p1_body, 0, unroll=P1_UNROLL)

    dil2 = DILATED_PATTERNS[1][1]
    na = nseg // dil2
    cq = blk // na
    nb2 = seg_len // cq

    def p2_body(t, carry):
        r4 = t // nb2
        j = t % nb2
        iq = pl.multiple_of(j * cq, cq)
        ik = pl.multiple_of(jnp.maximum(j - 1, 0) * cq, cq)
        segs = [dil2 * a + r4 for a in range(na)]
        q01 = jnp.concatenate([qm_ref[h, sg, pl.ds(iq, cq), :] for h in range(2) for sg in segs], axis=0)
        kw = jnp.concatenate([k_ref[sg, pl.ds(ik, 2 * cq), :] for sg in segs], axis=0)
        v0, v1 = [jnp.concatenate([vm_ref[h, sg, pl.ds(ik, 2 * cq), :] for sg in segs], axis=0) for h in range(2)]
        acc, mx, den = attend(q01, kw, v0, v1, biast_ref[1, jnp.minimum(j, 1)])
        for a, sg in enumerate(segs):
            rows = slice(a * cq, (a + 1) * cq)
            acc_refs[1][sg, pl.ds(iq, cq), :] = acc[rows]
            m_refs[1][sg, pl.ds(iq, cq), :] = mx[rows]
            l_refs[1][sg, pl.ds(iq, cq), :] = den[rows]
        return carry

    lax.fori_loop(0, dil2 * nb2, p2_body, 0, unroll=P2_UNROLL)

    assert seg_len == 2 * blk and DILATED_PATTERNS[2][1] == nseg

    def p3_body(t, carry):
        r = t // 2
        jb = t % 2
        rows = pl.ds(pl.multiple_of(jb * blk, blk), blk)
        q01 = jnp.concatenate([qm_ref[0, r, rows, :], qm_ref[1, r, rows, :]], axis=0)
        acc, mx, den = attend(q01, k_ref[r], vm_ref[0, r], vm_ref[1, r], biast_ref[2, jb])
        acc_refs[2][r, rows, :] = acc
        m_refs[2][r, rows, :] = mx
        l_refs[2][r, rows, :] = den
        return carry

    lax.fori_loop(0, 2 * nseg, p3_body, 0, unroll=P3_UNROLL)

    def merge_body(r, carry):
        ms = [m_refs[p][r] for p in range(3)]
        mtop = jnp.maximum(jnp.maximum(ms[0], ms[1]), ms[2])
        num = jnp.zeros((seg_len, LANES), F32)
        den = jnp.zeros((seg_len, LANES), F32)
        for p in range(3):
            e = jnp.exp2(ms[p] - mtop)
            num = num + e * acc_refs[p][r]
            den = den + e * l_refs[p][r]
        onat_ref[pl.ds(r, seg_len, stride=nseg), :] = num / den * g_ref[r].astype(F32)
        return carry

    lax.fori_loop(0, nseg, merge_body, 0)

    def out_body(i, carry):
        rows = pl.ds(pl.multiple_of(i * OUT_ROWS, OUT_ROWS), OUT_ROWS)
        o_ref[rows, :] = onat_ref[rows, :].astype(BF16)
        return carry

    lax.fori_loop(0, nseg * seg_len // OUT_ROWS, out_body, 0)


def _attn_masks():
    blk = ATTN_BLOCK
    tables = []
    for _, dil in DILATED_PATTERNS:
        na = NSEG // dil
        cq = blk // na
        qi = np.arange(blk)
        kj = np.arange(2 * blk)
        pos_q = na * (qi % cq) + qi // cq
        pos_k = na * (kj % (2 * cq)) + kj // (2 * cq)
        variants = []
        for q_off in (0, blk):
            dist = (pos_q[None, :] + q_off) - pos_k[:, None]
            variants.append(np.where((dist >= 0) & (dist <= blk), 0.0, MASK_VALUE))
        tables.append(np.stack(variants))
    return jnp.asarray(np.stack(tables).astype(np.float32), dtype=BF16)


def _attention(qkvg, seq):
    bsz, nseg, seg_len, _ = qkvg.shape
    n_pairs = D_ATTN // LANES
    blk = ATTN_BLOCK
    eye2 = jnp.asarray(np.tile(np.eye(blk, dtype=np.float32), (2, 1)), dtype=BF16)

    def col_spec(which):
        return pl.BlockSpec((None, nseg, seg_len, LANES), lambda b, hp: (b, 0, 0, which * n_pairs + hp))

    seg_f32 = pltpu.VMEM((nseg, seg_len, LANES), F32)
    pair_f32 = pltpu.VMEM((2, nseg, seg_len, LANES), F32)
    pair_bf16 = pltpu.VMEM((2, nseg, seg_len, LANES), BF16)
    scratch = [pair_f32, seg_f32, pair_f32, pair_bf16, pair_bf16, pltpu.VMEM((seq, LANES), F32)] + [seg_f32] * 9
    return pl.pallas_call(
        _attn_kernel,
        grid=(bsz, n_pairs),
        in_specs=[col_spec(0), col_spec(1), col_spec(2), col_spec(3),
                  pl.BlockSpec((2 * blk, blk), lambda b, hp: (0, 0)),
                  pl.BlockSpec((len(DILATED_PATTERNS), 2, 2 * blk, blk), lambda b, hp: (0, 0, 0, 0))],
        out_specs=pl.BlockSpec((None, seq, LANES), lambda b, hp: (b, 0, hp)),
        out_shape=jax.ShapeDtypeStruct((bsz, seq, D_ATTN), BF16),
        scratch_shapes=scratch,
        compiler_params=pltpu.CompilerParams(
            dimension_semantics=("parallel", "parallel"), vmem_limit_bytes=VMEM_LIMIT),
        name="dilated_attn",
    )(qkvg, qkvg, qkvg, qkvg, eye2, _attn_masks())


SSD_T = 512


def _ssd_kernel(xbc_ref, z_ref, dt_ref, dtb_ref, alog_ref, dskip_ref,
                nw_ref, rexp_ref, tri_ref, y_ref, state_ref, ydiag_ref):
    @pl.when(pl.program_id(1) == 0)
    def _():
        state_ref[...] = jnp.zeros(state_ref.shape, F32)

    a_neg = -jnp.exp(alog_ref[...])
    li = lax.broadcasted_iota(jnp.int32, (CHUNK, CHUNK), 0)
    si = lax.broadcasted_iota(jnp.int32, (CHUNK, CHUNK), 1)
    causal = li >= si
    lane = lax.broadcasted_iota(jnp.int32, (CHUNK, LANES), 1)
    head0 = lane < SSM_HEAD_DIM
    tri = tri_ref[...]
    rexp = rexp_ref[...]

    def expand(*es):
        parts = []
        for e in es:
            hi = e.astype(BF16)
            parts.append(jnp.concatenate([hi, (e - hi.astype(F32)).astype(BF16)], axis=1))
        out = jnp.dot(jnp.concatenate(parts, axis=0), rexp, preferred_element_type=F32)
        return [out[i * CHUNK:(i + 1) * CHUNK] for i in range(len(es))]

    for c in range(SSD_T // CHUNK):
        r0 = c * CHUNK
        xs = xbc_ref[r0:r0 + CHUNK, :D_SSM].astype(F32)

        dt_in = dt_ref[r0:r0 + CHUNK, :] + dtb_ref[...]
        dt = jnp.maximum(dt_in, 0.0) + jnp.log1p(jnp.exp(-jnp.abs(dt_in)))
        a_dt = dt * a_neg
        a_cs = sum(jnp.dot(tri, part, preferred_element_type=F32) for part in _split3(a_dt))
        a_cs_t = a_cs.T
        a_last = a_cs[CHUNK - 1:CHUNK, :]
        exp_acs_x, dt_x, dstate_x = expand(jnp.exp(a_cs), dt, jnp.exp(a_last - a_cs))
        cdecay_x = exp_acs_x[CHUNK - 1:CHUNK, :]

        xdt = xs * dt_x
        xdt_b = xdt.astype(BF16)
        xdts_b = (xdt * dstate_x).astype(BF16)

        for g in range(SSM_GROUPS):
            gcols = slice(g * GROUP_W, (g + 1) * GROUP_W)
            b_b = xbc_ref[r0:r0 + CHUNK, D_SSM + g * D_STATE:D_SSM + (g + 1) * D_STATE]
            c_b = xbc_ref[r0:r0 + CHUNK, D_SSM + (SSM_GROUPS + g) * D_STATE:D_SSM + (SSM_GROUPS + g + 1) * D_STATE]
            b_f = b_b.astype(F32)
            cb = lax.dot_general(c_b, b_b, (((1,), (1,)), ((), ())), preferred_element_type=F32)
            s_prev = state_ref[g]
            y_off = jnp.dot(c_b, s_prev.astype(BF16), preferred_element_type=F32) * exp_acs_x[:, gcols]
            new_state = jnp.dot(b_f.T.astype(BF16), xdts_b[:, gcols], preferred_element_type=F32)
            state_ref[g] = s_prev * cdecay_x[:, gcols] + new_state
            for hp in range(HEADS_PER_GROUP // 2):
                pair = []
                pcols = slice(g * GROUP_W + hp * LANES, g * GROUP_W + (hp + 1) * LANES)
                for e in range(2):
                    h = g * HEADS_PER_GROUP + 2 * hp + e
                    seg = a_cs[:, h:h + 1] - a_cs_t[h:h + 1, :]
                    decay = jnp.exp(jnp.where(causal, seg, -jnp.inf))
                    pair.append(jnp.dot((cb * decay).astype(BF16), xdt_b[:, pcols], preferred_element_type=F32))
                ydiag_ref[:, pcols] = jnp.where(head0, pair[0], pair[1])
            y = ydiag_ref[:, gcols] + y_off + dskip_ref[:, gcols] * xs[:, gcols]
            yz = y * z_ref[r0:r0 + CHUNK, gcols].astype(F32)
            ms = jnp.mean(yz * yz, axis=-1, keepdims=True)
            y_ref[r0:r0 + CHUNK, gcols] = (yz * lax.rsqrt(ms + NORM_EPS) * nw_ref[:, gcols]).astype(BF16)


def _ssd(xbc3, z3, dt3, dt_bias, a_log, d_skip, ssm_norm_w):
    bsz, seq, _ = xbc3.shape
    pad = LANES - SSM_HEADS
    dtb = jnp.pad(dt_bias, (0, pad))[None, :]
    alog = jnp.pad(a_log, (0, pad))[None, :]
    dskip = jnp.repeat(d_skip, SSM_HEAD_DIM)[None, :]
    rexp = (np.arange(LANES)[:, None] == (np.arange(D_SSM)[None, :] // SSM_HEAD_DIM))
    rexp = jnp.asarray(np.tile(rexp.astype(np.float32), (2, 1)), dtype=BF16)
    tri = jnp.asarray(np.tril(np.ones((CHUNK, CHUNK), np.float32)), dtype=BF16)

    def const(shape):
        return pl.BlockSpec(shape, lambda b, i: (0,) * len(shape))

    return pl.pallas_call(
        _ssd_kernel,
        grid=(bsz, seq // SSD_T),
        in_specs=[
            pl.BlockSpec((None, SSD_T, D_CONV), lambda b, i: (b, i, 0)),
            pl.BlockSpec((None, SSD_T, D_SSM), lambda b, i: (b, i, 0)),
            pl.BlockSpec((None, SSD_T, LANES), lambda b, i: (b, i, 0)),
            const((1, LANES)), const((1, LANES)),
            const((1, D_SSM)), const((1, D_SSM)), const((2 * LANES, D_SSM)), const((CHUNK, CHUNK)),
        ],
        out_specs=pl.BlockSpec((None, SSD_T, D_SSM), lambda b, i: (b, i, 0)),
        out_shape=jax.ShapeDtypeStruct((bsz, seq, D_SSM), BF16),
        scratch_shapes=[
            pltpu.VMEM((SSM_GROUPS, D_STATE, GROUP_W), F32),
            pltpu.VMEM((CHUNK, D_SSM), F32),
        ],
        compiler_params=pltpu.CompilerParams(
            dimension_semantics=("parallel", "arbitrary"), vmem_limit_bytes=VMEM_LIMIT),
        name="ssd",
    )(xbc3, z3, dt3, dtb, alog, dskip, ssm_norm_w[None, :], rexp, tri)


OUT_TM = 512


def _outproj_kernel(attn_ref, y_ref, x_ref, w_ref, nw_ref, o_ref):
    out = jnp.dot(attn_ref[...], w_ref[:D_ATTN, :], preferred_element_type=F32)
    out = out + jnp.dot(y_ref[...], w_ref[D_ATTN:, :], preferred_element_type=F32)
    ms = jnp.mean(out * out, axis=-1, keepdims=True)
    o_ref[...] = x_ref[...] + out * lax.rsqrt(ms + NORM_EPS) * nw_ref[...]


def _outproj(attn2, y2, x2, w_out_b, norm_w):
    m = x2.shape[0]
    return pl.pallas_call(
        _outproj_kernel,
        grid=(m // OUT_TM,),
        in_specs=[
            pl.BlockSpec((OUT_TM, D_ATTN), lambda i: (i, 0)),
            pl.BlockSpec((OUT_TM, D_SSM), lambda i: (i, 0)),
            pl.BlockSpec((OUT_TM, D_MODEL), lambda i: (i, 0)),
            pl.BlockSpec((D_ATTN + D_SSM, D_MODEL), lambda i: (0, 0)),
            pl.BlockSpec((1, D_MODEL), lambda i: (0, 0)),
        ],
        out_specs=pl.BlockSpec((OUT_TM, D_MODEL), lambda i: (i, 0)),
        out_shape=jax.ShapeDtypeStruct((m, D_MODEL), F32),
        compiler_params=pltpu.CompilerParams(
            dimension_semantics=("parallel",), vmem_limit_bytes=VMEM_LIMIT),
        name="outproj",
    )(attn2, y2, x2, w_out_b, norm_w)


def _layer(hid, norm_pre_w, w_in, conv_w, conv_b, dt_bias, a_log, d_skip, ssm_norm_w, w_out, norm_post_w):
    bsz, seq, _ = hid.shape
    assert hid.shape[-1] == D_MODEL and w_in.shape == (D_MODEL, D_IN_PROJ)
    assert seq == 2 * ATTN_BLOCK * NSEG and seq % SSD_T == 0 and (bsz * seq) % OUT_TM == 0
    w_b = w_in.astype(BF16)
    w_ssm = jnp.pad(w_b[:, D_QKVG:], ((0, 0), (0, D_SSM_PAD - (D_IN_PROJ - D_QKVG))))
    q_scale = ATTN_HEAD_DIM ** -0.5 * np.log2(np.e)
    w_attn = jnp.concatenate([(w_in[:, :D_ATTN] * q_scale).astype(BF16), w_b[:, D_ATTN:D_QKVG]], axis=1)
    qkvg, z, xbc, dt_raw = _inproj(hid, norm_pre_w[None, :], w_attn, w_ssm, conv_w, conv_b[None, :])
    attn = _attention(qkvg, seq)
    y = _ssd(xbc, z, dt_raw, dt_bias, a_log, d_skip, ssm_norm_w)
    x2 = hid.reshape(bsz * seq, D_MODEL)
    out = _outproj(attn.reshape(bsz * seq, D_ATTN), y.reshape(bsz * seq, D_SSM), x2,
                   w_out.astype(BF16), norm_post_w[None, :])
    return out.reshape(bsz, seq, D_MODEL)


def kernel(x, norm_pre_w, w_in, conv_w, conv_b, dt_bias, a_log, d_skip, ssm_norm_w, w_out, norm_post_w):
    hid = x
    for layer in range(norm_pre_w.shape[0]):
        hid = _layer(hid, norm_pre_w[layer], w_in[layer], conv_w[layer], conv_b[layer],
                     dt_bias[layer], a_log[layer], d_skip[layer], ssm_norm_w[layer],
                     w_out[layer], norm_post_w[layer])
    return hid
```

```python
import functools

import jax
import jax.numpy as jnp
import numpy as np
from jax import lax
from jax.experimental import pallas as pl
from jax.experimental.pallas import tpu as pltpu

D_MODEL = 1024
ATTN_HEADS = 16
ATTN_HEAD_DIM = 64
D_ATTN = ATTN_HEADS * ATTN_HEAD_DIM
DILATED_PATTERNS = ((128, 1), (512, 4), (2048, 16))
ATTN_BLOCK = 128
D_SSM = 1024
SSM_HEAD_DIM = 64
SSM_HEADS = D_SSM // SSM_HEAD_DIM
SSM_GROUPS = 2
D_STATE = 128
CONV_K = 4
CHUNK = 128
D_CONV = D_SSM + 2 * SSM_GROUPS * D_STATE
D_QKVG = 4 * D_ATTN
D_IN_PROJ = D_QKVG + D_SSM + D_CONV + SSM_HEADS
NORM_EPS = 1e-6

LANES = 128
SUBLANES = 8
D_SSM_PAD = D_SSM + D_CONV + LANES
GROUP_W = D_SSM // SSM_GROUPS
HEADS_PER_GROUP = SSM_HEADS // SSM_GROUPS
MASK_VALUE = -1e30
VMEM_LIMIT = 56 * 1024 * 1024

NSEG = max(d for _, d in DILATED_PATTERNS)

F32 = jnp.float32
BF16 = jnp.bfloat16


def _silu(x):
    h = 0.5 * x
    return h + h * jnp.tanh(h)


def _split3(x):
    hi = x.astype(BF16)
    r1 = x - hi.astype(F32)
    mid = r1.astype(BF16)
    lo = (r1 - mid.astype(F32)).astype(BF16)
    return hi, mid, lo


IN_CHUNK = 512
IN_TM = 512
IN_ROWS = IN_TM // NSEG
IN_PREV = 16


def _inproj_kernel(x_ref, xprev_ref, nw_ref, wa_ref, ws_ref, cw_ref, cb_ref,
                   qkvg_ref, z_ref, xbc_ref, dt_ref, slab_ref, ur_ref, cx_ref):
    def normed(xf):
        ms = jnp.mean(xf * xf, axis=-1, keepdims=True)
        return xf * lax.rsqrt(ms + NORM_EPS) * nw_ref[...]

    uf = normed(x_ref[...])
    un = uf.astype(BF16)
    up = normed(xprev_ref[...]).astype(BF16)
    keep_prev = (pl.program_id(1) > 0).astype(F32)

    n_slabs = D_MODEL // LANES
    for c in range(n_slabs):
        slab_ref[c] = uf[:, c * LANES:(c + 1) * LANES]

    def proj(c0, width):
        return jnp.dot(un, ws_ref[:, c0:c0 + width], preferred_element_type=F32)

    def z_chunk(ci):
        c0 = ci * IN_CHUNK
        zc = proj(c0, IN_CHUNK)
        z_ref[:, c0:c0 + IN_CHUNK] = _silu(zc).astype(BF16)

    def conv_chunk(ci):
        c0 = ci * IN_CHUNK
        wcols = slice(D_SSM + c0, D_SSM + c0 + IN_CHUNK)
        cx_ref[ci, 0:IN_PREV, :] = jnp.dot(up, ws_ref[:, wcols], preferred_element_type=F32) * keep_prev
        cx_ref[ci, IN_PREV:, :] = proj(D_SSM + c0, IN_CHUNK)
        ext = cx_ref[ci]
        conv = cb_ref[:, c0:c0 + IN_CHUNK] + cw_ref[CONV_K - 1:CONV_K, c0:c0 + IN_CHUNK] * ext
        for back in range(1, CONV_K):
            tap = cw_ref[CONV_K - 1 - back:CONV_K - back, c0:c0 + IN_CHUNK]
            conv = conv + tap * pltpu.roll(ext, back, axis=0)
        conv = conv[IN_PREV:, :]
        xbc_ref[:, c0:c0 + IN_CHUNK] = _silu(conv).astype(BF16)

    def regroup():
        for c in range(n_slabs):
            for r in range(NSEG):
                ur_ref[r * IN_ROWS:(r + 1) * IN_ROWS, c * LANES:(c + 1) * LANES] = (
                    slab_ref[c, pl.ds(r, IN_ROWS, stride=NSEG), :].astype(BF16))

    def attn_chunk(ci):
        c0 = ci * IN_CHUNK
        res = jnp.dot(ur_ref[...], wa_ref[:, c0:c0 + IN_CHUNK], preferred_element_type=F32)
        if c0 >= 3 * D_ATTN:
            res = _silu(res)
        res = res.astype(BF16)
        for r in range(NSEG):
            qkvg_ref[r, :, c0:c0 + IN_CHUNK] = res[r * IN_ROWS:(r + 1) * IN_ROWS, :]

    n_attn = D_QKVG // IN_CHUNK
    n_gate = D_ATTN // IN_CHUNK
    heavy = ([(conv_chunk, ci) for ci in range(D_CONV // IN_CHUNK)] + [(z_chunk, ci) for ci in range(D_SSM // IN_CHUNK)]
             + [(attn_chunk, ci) for ci in range(n_attn - n_gate, n_attn)])
    plain = [(attn_chunk, ci) for ci in range(n_attn - n_gate)]
    heavy[0][0](heavy[0][1])
    regroup()
    rest = heavy[1:]
    while rest or plain:
        if plain:
            fn, ci = plain.pop(0)
            fn(ci)
        if rest:
            fn, ci = rest.pop(0)
            fn(ci)
    dt_ref[...] = proj(D_SSM + D_CONV, LANES)


def _inproj(x, norm_w, w_attn, w_ssm, conv_w, conv_b):
    bsz, seq, _ = x.shape
    seg_len = seq // NSEG
    prev_per_tile = IN_TM // IN_PREV
    return pl.pallas_call(
        _inproj_kernel,
        grid=(bsz, seq // IN_TM),
        in_specs=[
            pl.BlockSpec((None, IN_TM, D_MODEL), lambda b, k: (b, k, 0)),
            pl.BlockSpec((None, IN_PREV, D_MODEL), lambda b, k: (b, jnp.maximum(k * prev_per_tile - 1, 0), 0)),
            pl.BlockSpec((1, D_MODEL), lambda b, k: (0, 0)),
            pl.BlockSpec((D_MODEL, D_QKVG), lambda b, k: (0, 0), pipeline_mode=pl.Buffered(1)),
            pl.BlockSpec((D_MODEL, D_SSM_PAD), lambda b, k: (0, 0), pipeline_mode=pl.Buffered(1)),
            pl.BlockSpec((CONV_K, D_CONV), lambda b, k: (0, 0)),
            pl.BlockSpec((1, D_CONV), lambda b, k: (0, 0)),
        ],
        out_specs=[
            pl.BlockSpec((None, NSEG, IN_ROWS, D_QKVG), lambda b, k: (b, 0, k, 0)),
            pl.BlockSpec((None, IN_TM, D_SSM), lambda b, k: (b, k, 0)),
            pl.BlockSpec((None, IN_TM, D_CONV), lambda b, k: (b, k, 0)),
            pl.BlockSpec((None, IN_TM, LANES), lambda b, k: (b, k, 0)),
        ],
        out_shape=[
            jax.ShapeDtypeStruct((bsz, NSEG, seg_len, D_QKVG), BF16),
            jax.ShapeDtypeStruct((bsz, seq, D_SSM), BF16),
            jax.ShapeDtypeStruct((bsz, seq, D_CONV), BF16),
            jax.ShapeDtypeStruct((bsz, seq, LANES), F32),
        ],
        scratch_shapes=[
            pltpu.VMEM((D_MODEL // LANES, IN_TM, LANES), F32),
            pltpu.VMEM((IN_TM, D_MODEL), BF16),
            pltpu.VMEM((D_CONV // IN_CHUNK, IN_PREV + IN_TM, IN_CHUNK), F32),
        ],
        compiler_params=pltpu.CompilerParams(
            dimension_semantics=("parallel", "parallel"), vmem_limit_bytes=VMEM_LIMIT),
        name="inproj",
    )(x, x, norm_w, w_attn, w_ssm, conv_w, conv_b)


P1_UNROLL = 32
P2_UNROLL = 32
P3_UNROLL = 32
OUT_ROWS = 512


def _attn_kernel(q_ref, k_ref, v_ref, g_ref, eye_ref, biast_ref, o_ref,
                 qf_ref, kf_ref, vf_ref, qm_ref, vm_ref, onat_ref, *stat_refs):
    acc_refs, m_refs, l_refs = stat_refs[0:3], stat_refs[3:6], stat_refs[6:9]
    nseg, seg_len, _ = q_ref.shape
    blk = ATTN_BLOCK
    head0 = lax.broadcasted_iota(jnp.int32, (blk, LANES), 1) < ATTN_HEAD_DIM
    head0_seg = lax.broadcasted_iota(jnp.int32, (seg_len, LANES), 1) < ATTN_HEAD_DIM
    head0_win = lax.broadcasted_iota(jnp.int32, (2 * blk, LANES), 1) < ATTN_HEAD_DIM
    ones_h = (jnp.where(head0_win, 1.0, 0.0).astype(BF16), jnp.where(head0_win, 0.0, 1.0).astype(BF16))

    def prep(r, carry):
        qf = q_ref[r].astype(F32)
        vf = v_ref[r].astype(F32)
        for h in range(2):
            keep = head0_seg if h == 0 else jnp.logical_not(head0_seg)
            qh = jnp.where(keep, qf, 0.0)
            vh = jnp.where(keep, vf, 0.0)
            qf_ref[h, r] = qh
            vf_ref[h, r] = vh
            qm_ref[h, r] = qh.astype(BF16)
            vm_ref[h, r] = vh.astype(BF16)
        kf_ref[r] = k_ref[r].astype(F32)
        return carry

    lax.fori_loop(0, nseg, prep, 0)

    def attend(q01, kw, v0, v1, bias_t):
        qa = jnp.concatenate([q01, eye_ref[...]], axis=1)
        ka = jnp.concatenate([kw, bias_t], axis=1)
        s = lax.dot_general(qa, ka, (((1,), (1,)), ((), ())), preferred_element_type=F32)
        m = jnp.max(s, axis=-1, keepdims=True)
        pr = jnp.exp2(s - m).astype(BF16)
        lhs = jnp.concatenate([pr[:blk], pr[blk:]], axis=1)
        rhs = jnp.concatenate([jnp.concatenate([v0, ones_h[0]], axis=1),
                               jnp.concatenate([v1, ones_h[1]], axis=1)], axis=0)
        oa = jnp.dot(lhs, rhs, preferred_element_type=F32)
        mb = jnp.broadcast_to(m, (2 * blk, LANES))
        return oa[:, :LANES], jnp.where(head0, mb[:blk], mb[blk:]), oa[:, LANES:]

    def p1_body(j, carry):
        iq = pl.multiple_of(j * SUBLANES, SUBLANES)
        ik = pl.multiple_of(jnp.maximum(j - 1, 0) * SUBLANES, SUBLANES)
        q01 = jnp.concatenate([qf_ref[h, r, pl.ds(iq, SUBLANES), :] for h in range(2) for r in range(nseg)],
                              axis=0).astype(BF16)
        kw = jnp.concatenate([kf_ref[r, pl.ds(ik, 2 * SUBLANES), :] for r in range(nseg)], axis=0).astype(BF16)
        v0, v1 = [jnp.concatenate([vf_ref[h, r, pl.ds(ik, 2 * SUBLANES), :] for r in range(nseg)],
                                  axis=0).astype(BF16) for h in range(2)]
        acc, mx, den = attend(q01, kw, v0, v1, biast_ref[0, jnp.minimum(j, 1)])
        for r in range(nseg):
            rows = slice(r * SUBLANES, (r + 1) * SUBLANES)
            acc_refs[0][r, pl.ds(iq, SUBLANES), :] = acc[rows]
            m_refs[0][r, pl.ds(iq, SUBLANES), :] = mx[rows]
            l_refs[0][r, pl.ds(iq, SUBLANES), :] = den[rows]
        return carry

    lax.fori_loop(0, seg_len // SUBLANES, p1_body, 0, unroll=P1_UNROLL)

    dil2 = DILATED_PATTERNS[1][1]
    na = nseg // dil2
    cq = blk // na
    nb2 = seg_len // cq

    def p2_body(t, carry):
        r4 = t // nb2
        j = t % nb2
        iq = pl.multiple_of(j * cq, cq)
        ik = pl.multiple_of(jnp.maximum(j - 1, 0) * cq, cq)
        segs = [dil2 * a + r4 for a in range(na)]
        q01 = jnp.concatenate([qm_ref[h, sg, pl.ds(iq, cq), :] for h in range(2) for sg in segs], axis=0)
        kw = jnp.concatenate([k_ref[sg, pl.ds(ik, 2 * cq), :] for sg in segs], axis=0)
        v0, v1 = [jnp.concatenate([vm_ref[h, sg, pl.ds(ik, 2 * cq), :] for sg in segs], axis=0) for h in range(2)]
        acc, mx, den = attend(q01, kw, v0, v1, biast_ref[1, jnp.minimum(j, 1)])
        for a, sg in enumerate(segs):
            rows = slice(a * cq, (a + 1) * cq)
            acc_refs[1][sg, pl.ds(iq, cq), :] = acc[rows]
            m_refs[1][sg, pl.ds(iq, cq), :] = mx[rows]
            l_refs[1][sg, pl.ds(iq, cq), :] = den[rows]
        return carry

    lax.fori_loop(0, dil2 * nb2, p2_body, 0, unroll=P2_UNROLL)

    assert seg_len == 2 * blk and DILATED_PATTERNS[2][1] == nseg

    def p3_body(t, carry):
        r = t // 2
        jb = t % 2
        rows = pl.ds(pl.multiple_of(jb * blk, blk), blk)
        q01 = jnp.concatenate([qm_ref[0, r, rows, :], qm_ref[1, r, rows, :]], axis=0)
        acc, mx, den = attend(q01, k_ref[r], vm_ref[0, r], vm_ref[1, r], biast_ref[2, jb])
        acc_refs[2][r, rows, :] = acc
        m_refs[2][r, rows, :] = mx
        l_refs[2][r, rows, :] = den
        return carry

    lax.fori_loop(0, 2 * nseg, p3_body, 0, unroll=P3_UNROLL)

    def merge_body(r, carry):
        ms = [m_refs[p][r] for p in range(3)]
        mtop = jnp.maximum(jnp.maximum(ms[0], ms[1]), ms[2])
        num = jnp.zeros((seg_len, LANES), F32)
        den = jnp.zeros((seg_len, LANES), F32)
        for p in range(3):
            e = jnp.exp2(ms[p] - mtop)
            num = num + e * acc_refs[p][r]
            den = den + e * l_refs[p][r]
        onat_ref[pl.ds(r, seg_len, stride=nseg), :] = num / den * g_ref[r].astype(F32)
        return carry

    lax.fori_loop(0, nseg, merge_body, 0)

    def out_body(i, carry):
        rows = pl.ds(pl.multiple_of(i * OUT_ROWS, OUT_ROWS), OUT_ROWS)
        o_ref[rows, :] = onat_ref[rows, :].astype(BF16)
        return carry

    lax.fori_loop(0, nseg * seg_len // OUT_ROWS, out_body, 0)


def _attn_masks():
    blk = ATTN_BLOCK
    tables = []
    for _, dil in DILATED_PATTERNS:
        na = NSEG // dil
        cq = blk // na
        qi = np.arange(blk)
        kj = np.arange(2 * blk)
        pos_q = na * (qi % cq) + qi // cq
        pos_k = na * (kj % (2 * cq)) + kj // (2 * cq)
        variants = []
        for q_off in (0, blk):
            dist = (pos_q[None, :] + q_off) - pos_k[:, None]
            variants.append(np.where((dist >= 0) & (dist <= blk), 0.0, MASK_VALUE))
        tables.append(np.stack(variants))
    return jnp.asarray(np.stack(tables).astype(np.float32), dtype=BF16)


def _attention(qkvg, seq):
    bsz, nseg, seg_len, _ = qkvg.shape
    n_pairs = D_ATTN // LANES
    blk = ATTN_BLOCK
    eye2 = jnp.asarray(np.tile(np.eye(blk, dtype=np.float32), (2, 1)), dtype=BF16)

    def col_spec(which):
        return pl.BlockSpec((None, nseg, seg_len, LANES), lambda b, hp: (b, 0, 0, which * n_pairs + hp))

    seg_f32 = pltpu.VMEM((nseg, seg_len, LANES), F32)
    pair_f32 = pltpu.VMEM((2, nseg, seg_len, LANES), F32)
    pair_bf16 = pltpu.VMEM((2, nseg, seg_len, LANES), BF16)
    scratch = [pair_f32, seg_f32, pair_f32, pair_bf16, pair_bf16, pltpu.VMEM((seq, LANES), F32)] + [seg_f32] * 9
    return pl.pallas_call(
        _attn_kernel,
        grid=(bsz, n_pairs),
        in_specs=[col_spec(0), col_spec(1), col_spec(2), col_spec(3),
                  pl.BlockSpec((2 * blk, blk), lambda b, hp: (0, 0)),
                  pl.BlockSpec((len(DILATED_PATTERNS), 2, 2 * blk, blk), lambda b, hp: (0, 0, 0, 0))],
        out_specs=pl.BlockSpec((None, seq, LANES), lambda b, hp: (b, 0, hp)),
        out_shape=jax.ShapeDtypeStruct((bsz, seq, D_ATTN), BF16),
        scratch_shapes=scratch,
        compiler_params=pltpu.CompilerParams(
            dimension_semantics=("parallel", "parallel"), vmem_limit_bytes=VMEM_LIMIT),
        name="dilated_attn",
    )(qkvg, qkvg, qkvg, qkvg, eye2, _attn_masks())


SSD_T = 512


def _ssd_kernel(xbc_ref, z_ref, dt_ref, dtb_ref, alog_ref, dskip_ref,
                nw_ref, rexp_ref, tri_ref, y_ref, state_ref, ydiag_ref):
    @pl.when(pl.program_id(1) == 0)
    def _():
        state_ref[...] = jnp.zeros(state_ref.shape, F32)

    a_neg = -jnp.exp(alog_ref[...])
    li = lax.broadcasted_iota(jnp.int32, (CHUNK, CHUNK), 0)
    si = lax.broadcasted_iota(jnp.int32, (CHUNK, CHUNK), 1)
    causal = li >= si
    chan = lax.broadcasted_iota(jnp.int32, (CHUNK, D_SSM), 1)
    even_head = (chan // SSM_HEAD_DIM) % 2 == 0
    tri = tri_ref[...]
    rexp = rexp_ref[...]

    def expand(*es):
        parts = []
        for e in es:
            hi = e.astype(BF16)
            parts.append(jnp.concatenate([hi, (e - hi.astype(F32)).astype(BF16)], axis=1))
        out = jnp.dot(jnp.concatenate(parts, axis=0), rexp, preferred_element_type=F32)
        return [out[i * CHUNK:(i + 1) * CHUNK] for i in range(len(es))]

    for c in range(SSD_T // CHUNK):
        r0 = c * CHUNK
        xs = xbc_ref[r0:r0 + CHUNK, :D_SSM].astype(F32)

        dt_in = dt_ref[r0:r0 + CHUNK, :] + dtb_ref[...]
        dt = jnp.maximum(dt_in, 0.0) + jnp.log1p(jnp.exp(-jnp.abs(dt_in)))
        a_dt = dt * a_neg
        a_cs = sum(jnp.dot(tri, part, preferred_element_type=F32) for part in _split3(a_dt))
        a_cs_t = a_cs.T
        a_last = a_cs[CHUNK - 1:CHUNK, :]
        exp_acs_x, dt_x, dstate_x = expand(jnp.exp(a_cs), dt, jnp.exp(a_last - a_cs))
        cdecay_x = exp_acs_x[CHUNK - 1:CHUNK, :]

        xdt = xs * dt_x
        xdt_pair = (jnp.where(even_head, xdt, 0.0).astype(BF16), jnp.where(even_head, 0.0, xdt).astype(BF16))
        xdts_b = (xdt * dstate_x).astype(BF16)

        for g in range(SSM_GROUPS):
            gcols = slice(g * GROUP_W, (g + 1) * GROUP_W)
            b_b = xbc_ref[r0:r0 + CHUNK, D_SSM + g * D_STATE:D_SSM + (g + 1) * D_STATE]
            c_b = xbc_ref[r0:r0 + CHUNK, D_SSM + (SSM_GROUPS + g) * D_STATE:D_SSM + (SSM_GROUPS + g + 1) * D_STATE]
            b_f = b_b.astype(F32)
            cb = lax.dot_general(c_b, b_b, (((1,), (1,)), ((), ())), preferred_element_type=F32)
            s_prev = state_ref[g]
            y_off = jnp.dot(c_b, s_prev.astype(BF16), preferred_element_type=F32) * exp_acs_x[:, gcols]
            new_state = jnp.dot(b_f.T.astype(BF16), xdts_b[:, gcols], preferred_element_type=F32)
            state_ref[g] = s_prev * cdecay_x[:, gcols] + new_state
            for hp in range(HEADS_PER_GROUP // 2):
                pcols = slice(g * GROUP_W + hp * LANES, g * GROUP_W + (hp + 1) * LANES)
                mats = []
                for e in range(2):
                    h = g * HEADS_PER_GROUP + 2 * hp + e
                    seg = a_cs[:, h:h + 1] - a_cs_t[h:h + 1, :]
                    decay = jnp.exp(jnp.where(causal, seg, -jnp.inf))
                    mats.append((cb * decay).astype(BF16))
                ydiag_ref[:, pcols] = jnp.dot(jnp.concatenate(mats, axis=1),
                                              jnp.concatenate([xp[:, pcols] for xp in xdt_pair], axis=0),
                                              preferred_element_type=F32)
            y = ydiag_ref[:, gcols] + y_off + dskip_ref[:, gcols] * xs[:, gcols]
            yz = y * z_ref[r0:r0 + CHUNK, gcols].astype(F32)
            ms = jnp.mean(yz * yz, axis=-1, keepdims=True)
            y_ref[r0:r0 + CHUNK, gcols] = (yz * lax.rsqrt(ms + NORM_EPS) * nw_ref[:, gcols]).astype(BF16)


def _ssd(xbc3, z3, dt3, dt_bias, a_log, d_skip, ssm_norm_w):
    bsz, seq, _ = xbc3.shape
    pad = LANES - SSM_HEADS
    dtb = jnp.pad(dt_bias, (0, pad))[None, :]
    alog = jnp.pad(a_log, (0, pad))[None, :]
    dskip = jnp.repeat(d_skip, SSM_HEAD_DIM)[None, :]
    rexp = (np.arange(LANES)[:, None] == (np.arange(D_SSM)[None, :] // SSM_HEAD_DIM))
    rexp = jnp.asarray(np.tile(rexp.astype(np.float32), (2, 1)), dtype=BF16)
    tri = jnp.asarray(np.tril(np.ones((CHUNK, CHUNK), np.float32)), dtype=BF16)

    def const(shape):
        return pl.BlockSpec(shape, lambda b, i: (0,) * len(shape))

    return pl.pallas_call(
        _ssd_kernel,
        grid=(bsz, seq // SSD_T),
        in_specs=[
            pl.BlockSpec((None, SSD_T, D_CONV), lambda b, i: (b, i, 0)),
            pl.BlockSpec((None, SSD_T, D_SSM), lambda b, i: (b, i, 0)),
            pl.BlockSpec((None, SSD_T, LANES), lambda b, i: (b, i, 0)),
            const((1, LANES)), const((1, LANES)),
            const((1, D_SSM)), const((1, D_SSM)), const((2 * LANES, D_SSM)), const((CHUNK, CHUNK)),
        ],
        out_specs=pl.BlockSpec((None, SSD_T, D_SSM), lambda b, i: (b, i, 0)),
        out_shape=jax.ShapeDtypeStruct((bsz, seq, D_SSM), BF16),
        scratch_shapes=[
            pltpu.VMEM((SSM_GROUPS, D_STATE, GROUP_W), F32),
            pltpu.VMEM((CHUNK, D_SSM), F32),
        ],
        compiler_params=pltpu.CompilerParams(
            dimension_semantics=("parallel", "arbitrary"), vmem_limit_bytes=VMEM_LIMIT),
        name="ssd",
    )(xbc3, z3, dt3, dtb, alog, dskip, ssm_norm_w[None, :], rexp, tri)


OUT_TM = 512


def _outproj_kernel(attn_ref, y_ref, x_ref, w_ref, nw_ref, o_ref):
    out = jnp.dot(attn_ref[...], w_ref[:D_ATTN, :], preferred_element_type=F32)
    out = out + jnp.dot(y_ref[...], w_ref[D_ATTN:, :], preferred_element_type=F32)
    ms = jnp.mean(out * out, axis=-1, keepdims=True)
    o_ref[...] = x_ref[...] + out * lax.rsqrt(ms + NORM_EPS) * nw_ref[...]


def _outproj(attn2, y2, x2, w_out_b, norm_w):
    m = x2.shape[0]
    return pl.pallas_call(
        _outproj_kernel,
        grid=(m // OUT_TM,),
        in_specs=[
            pl.BlockSpec((OUT_TM, D_ATTN), lambda i: (i, 0)),
            pl.BlockSpec((OUT_TM, D_SSM), lambda i: (i, 0)),
            pl.BlockSpec((OUT_TM, D_MODEL), lambda i: (i, 0)),
            pl.BlockSpec((D_ATTN + D_SSM, D_MODEL), lambda i: (0, 0)),
            pl.BlockSpec((1, D_MODEL), lambda i: (0, 0)),
        ],
        out_specs=pl.BlockSpec((OUT_TM, D_MODEL), lambda i: (i, 0)),
        out_shape=jax.ShapeDtypeStruct((m, D_MODEL), F32),
        compiler_params=pltpu.CompilerParams(
            dimension_semantics=("parallel",), vmem_limit_bytes=VMEM_LIMIT),
        name="outproj",
    )(attn2, y2, x2, w_out_b, norm_w)


def _layer(hid, norm_pre_w, w_in, conv_w, conv_b, dt_bias, a_log, d_skip, ssm_norm_w, w_out, norm_post_w):
    bsz, seq, _ = hid.shape
    assert hid.shape[-1] == D_MODEL and w_in.shape == (D_MODEL, D_IN_PROJ)
    assert seq == 2 * ATTN_BLOCK * NSEG and seq % SSD_T == 0 and (bsz * seq) % OUT_TM == 0
    w_ssm = jnp.pad(w_in[:, D_QKVG:], ((0, 0), (0, D_SSM_PAD - (D_IN_PROJ - D_QKVG)))).astype(BF16)
    q_scale = ATTN_HEAD_DIM ** -0.5 * np.log2(np.e)
    col_scale = np.where(np.arange(D_QKVG) < D_ATTN, q_scale, 1.0).astype(np.float32)
    w_attn = (w_in[:, :D_QKVG] * col_scale).astype(BF16)
    qkvg, z, xbc, dt_raw = _inproj(hid, norm_pre_w[None, :], w_attn, w_ssm, conv_w, conv_b[None, :])
    attn = _attention(qkvg, seq)
    y = _ssd(xbc, z, dt_raw, dt_bias, a_log, d_skip, ssm_norm_w)
    x2 = hid.reshape(bsz * seq, D_MODEL)
    out = _outproj(attn.reshape(bsz * seq, D_ATTN), y.reshape(bsz * seq, D_SSM), x2,
                   w_out.astype(BF16), norm_post_w[None, :])
    return out.reshape(bsz, seq, D_MODEL)


def kernel(x, norm_pre_w, w_in, conv_w, conv_b, dt_bias, a_log, d_skip, ssm_norm_w, w_out, norm_post_w):
    hid = x
    for layer in range(norm_pre_w.shape[0]):
        hid = _layer(hid, norm_pre_w[layer], w_in[layer], conv_w[layer], conv_b[layer],
                     dt_bias[layer], a_log[layer], d_skip[layer], ssm_norm_w[layer],
                     w_out[layer], norm_post_w[layer])
    return hid
```

```python
import functools

import jax
import jax.numpy as jnp
import numpy as np
from jax import lax
from jax.experimental import pallas as pl
from jax.experimental.pallas import tpu as pltpu

D_MODEL = 1024
ATTN_HEADS = 16
ATTN_HEAD_DIM = 64
D_ATTN = ATTN_HEADS * ATTN_HEAD_DIM
DILATED_PATTERNS = ((128, 1), (512, 4), (2048, 16))
ATTN_BLOCK = 128
D_SSM = 1024
SSM_HEAD_DIM = 64
SSM_HEADS = D_SSM // SSM_HEAD_DIM
SSM_GROUPS = 2
D_STATE = 128
CONV_K = 4
CHUNK = 128
D_CONV = D_SSM + 2 * SSM_GROUPS * D_STATE
D_QKVG = 4 * D_ATTN
D_IN_PROJ = D_QKVG + D_SSM + D_CONV + SSM_HEADS
NORM_EPS = 1e-6

LANES = 128
SUBLANES = 8
D_SSM_PAD = D_SSM + D_CONV + LANES
GROUP_W = D_SSM // SSM_GROUPS
HEADS_PER_GROUP = SSM_HEADS // SSM_GROUPS
MASK_VALUE = -1e30
VMEM_LIMIT = 56 * 1024 * 1024

NSEG = max(d for _, d in DILATED_PATTERNS)

F32 = jnp.float32
BF16 = jnp.bfloat16


def _silu(x):
    h = 0.5 * x
    return h + h * jnp.tanh(h)


def _split3(x):
    hi = x.astype(BF16)
    r1 = x - hi.astype(F32)
    mid = r1.astype(BF16)
    lo = (r1 - mid.astype(F32)).astype(BF16)
    return hi, mid, lo


IN_CHUNK = 512
IN_TM = 512
IN_ROWS = IN_TM // NSEG
IN_PREV = 16


def _inproj_kernel(x_ref, xprev_ref, nw_ref, wa_ref, ws_ref, cw_ref, cb_ref,
                   qkvg_ref, z_ref, xbc_ref, dt_ref, slab_ref, ur_ref, cx_ref):
    def normed(xf):
        ms = jnp.mean(xf * xf, axis=-1, keepdims=True)
        return xf * lax.rsqrt(ms + NORM_EPS) * nw_ref[...]

    uf = normed(x_ref[...])
    un = uf.astype(BF16)
    up = normed(xprev_ref[...]).astype(BF16)
    keep_prev = (pl.program_id(1) > 0).astype(F32)

    n_slabs = D_MODEL // LANES
    for c in range(n_slabs):
        slab_ref[c] = uf[:, c * LANES:(c + 1) * LANES]

    def proj(c0, width):
        return jnp.dot(un, ws_ref[:, c0:c0 + width], preferred_element_type=F32)

    def z_chunk(ci):
        c0 = ci * IN_CHUNK
        zc = proj(c0, IN_CHUNK)
        z_ref[:, c0:c0 + IN_CHUNK] = _silu(zc).astype(BF16)

    def conv_chunk(ci):
        c0 = ci * IN_CHUNK
        wcols = slice(D_SSM + c0, D_SSM + c0 + IN_CHUNK)
        cx_ref[ci, 0:IN_PREV, :] = jnp.dot(up, ws_ref[:, wcols], preferred_element_type=F32) * keep_prev
        cx_ref[ci, IN_PREV:, :] = proj(D_SSM + c0, IN_CHUNK)
        ext = cx_ref[ci]
        conv = cb_ref[:, c0:c0 + IN_CHUNK] + cw_ref[CONV_K - 1:CONV_K, c0:c0 + IN_CHUNK] * ext
        for back in range(1, CONV_K):
            tap = cw_ref[CONV_K - 1 - back:CONV_K - back, c0:c0 + IN_CHUNK]
            conv = conv + tap * pltpu.roll(ext, back, axis=0)
        conv = conv[IN_PREV:, :]
        xbc_ref[:, c0:c0 + IN_CHUNK] = _silu(conv).astype(BF16)

    def regroup():
        for c in range(n_slabs):
            for r in range(NSEG):
                ur_ref[r * IN_ROWS:(r + 1) * IN_ROWS, c * LANES:(c + 1) * LANES] = (
                    slab_ref[c, pl.ds(r, IN_ROWS, stride=NSEG), :].astype(BF16))

    def attn_chunk(ci):
        c0 = ci * IN_CHUNK
        res = jnp.dot(ur_ref[...], wa_ref[:, c0:c0 + IN_CHUNK], preferred_element_type=F32)
        if c0 >= 3 * D_ATTN:
            res = _silu(res)
        res = res.astype(BF16)
        for r in range(NSEG):
            qkvg_ref[r, :, c0:c0 + IN_CHUNK] = res[r * IN_ROWS:(r + 1) * IN_ROWS, :]

    n_attn = D_QKVG // IN_CHUNK
    n_gate = D_ATTN // IN_CHUNK
    heavy = ([(conv_chunk, ci) for ci in range(D_CONV // IN_CHUNK)] + [(z_chunk, ci) for ci in range(D_SSM // IN_CHUNK)]
             + [(attn_chunk, ci) for ci in range(n_attn - n_gate, n_attn)])
    plain = [(attn_chunk, ci) for ci in range(n_attn - n_gate)]
    heavy[0][0](heavy[0][1])
    regroup()
    rest = heavy[1:]
    while rest or plain:
        if plain:
            fn, ci = plain.pop(0)
            fn(ci)
        if rest:
            fn, ci = rest.pop(0)
            fn(ci)
    dt_ref[...] = proj(D_SSM + D_CONV, LANES)


def _inproj(x, norm_w, w_attn, w_ssm, conv_w, conv_b):
    bsz, seq, _ = x.shape
    seg_len = seq // NSEG
    prev_per_tile = IN_TM // IN_PREV
    return pl.pallas_call(
        _inproj_kernel,
        grid=(bsz, seq // IN_TM),
        in_specs=[
            pl.BlockSpec((None, IN_TM, D_MODEL), lambda b, k: (b, k, 0)),
            pl.BlockSpec((None, IN_PREV, D_MODEL), lambda b, k: (b, jnp.maximum(k * prev_per_tile - 1, 0), 0)),
            pl.BlockSpec((1, D_MODEL), lambda b, k: (0, 0)),
            pl.BlockSpec((D_MODEL, D_QKVG), lambda b, k: (0, 0), pipeline_mode=pl.Buffered(1)),
            pl.BlockSpec((D_MODEL, D_SSM_PAD), lambda b, k: (0, 0), pipeline_mode=pl.Buffered(1)),
            pl.BlockSpec((CONV_K, D_CONV), lambda b, k: (0, 0)),
            pl.BlockSpec((1, D_CONV), lambda b, k: (0, 0)),
        ],
        out_specs=[
            pl.BlockSpec((None, NSEG, IN_ROWS, D_QKVG), lambda b, k: (b, 0, k, 0)),
            pl.BlockSpec((None, IN_TM, D_SSM), lambda b, k: (b, k, 0)),
            pl.BlockSpec((None, IN_TM, D_CONV), lambda b, k: (b, k, 0)),
            pl.BlockSpec((None, IN_TM, LANES), lambda b, k: (b, k, 0)),
        ],
        out_shape=[
            jax.ShapeDtypeStruct((bsz, NSEG, seg_len, D_QKVG), BF16),
            jax.ShapeDtypeStruct((bsz, seq, D_SSM), BF16),
            jax.ShapeDtypeStruct((bsz, seq, D_CONV), BF16),
            jax.ShapeDtypeStruct((bsz, seq, LANES), F32),
        ],
        scratch_shapes=[
            pltpu.VMEM((D_MODEL // LANES, IN_TM, LANES), F32),
            pltpu.VMEM((IN_TM, D_MODEL), BF16),
            pltpu.VMEM((D_CONV // IN_CHUNK, IN_PREV + IN_TM, IN_CHUNK), F32),
        ],
        compiler_params=pltpu.CompilerParams(
            dimension_semantics=("parallel", "parallel"), vmem_limit_bytes=VMEM_LIMIT),
        name="inproj",
    )(x, x, norm_w, w_attn, w_ssm, conv_w, conv_b)


def _attn_kernel(q_ref, k_ref, v_ref, g_ref, eye_ref, biast_ref, o_ref,
                 qf_ref, kf_ref, vf_ref, gf_ref, qm_ref, vm_ref, onat_ref, *stat_refs):
    acc_refs, m_refs, l_refs = stat_refs[0:2], stat_refs[2:4], stat_refs[4:6]
    nseg, seg_len, _ = q_ref.shape
    blk = ATTN_BLOCK
    head0 = lax.broadcasted_iota(jnp.int32, (blk, LANES), 1) < ATTN_HEAD_DIM
    head0_seg = lax.broadcasted_iota(jnp.int32, (seg_len, LANES), 1) < ATTN_HEAD_DIM
    head0_win = lax.broadcasted_iota(jnp.int32, (2 * blk, LANES), 1) < ATTN_HEAD_DIM
    ones_h = (jnp.where(head0_win, 1.0, 0.0).astype(BF16), jnp.where(head0_win, 0.0, 1.0).astype(BF16))

    def prep(r):
        gf_ref[r] = g_ref[r].astype(F32)
        qf = q_ref[r].astype(F32)
        vf = v_ref[r].astype(F32)
        for h in range(2):
            keep = head0_seg if h == 0 else jnp.logical_not(head0_seg)
            qh = jnp.where(keep, qf, 0.0)
            vh = jnp.where(keep, vf, 0.0)
            qf_ref[h, r] = qh
            vf_ref[h, r] = vh
            qm_ref[h, r] = qh.astype(BF16)
            vm_ref[h, r] = vh.astype(BF16)
        kf_ref[r] = k_ref[r].astype(F32)

    def attend(q01, kw, v0, v1, bias_t):
        qa = jnp.concatenate([q01, eye_ref[...]], axis=1)
        ka = jnp.concatenate([kw, bias_t], axis=1)
        s = lax.dot_general(qa, ka, (((1,), (1,)), ((), ())), preferred_element_type=F32)
        m = jnp.max(s, axis=-1, keepdims=True)
        pr = jnp.exp2(s - m).astype(BF16)
        lhs = jnp.concatenate([pr[:blk], pr[blk:]], axis=1)
        rhs = jnp.concatenate([jnp.concatenate([v0, ones_h[0]], axis=1),
                               jnp.concatenate([v1, ones_h[1]], axis=1)], axis=0)
        oa = jnp.dot(lhs, rhs, preferred_element_type=F32)
        mb = jnp.broadcast_to(m, (2 * blk, LANES))
        return oa[:, :LANES], jnp.where(head0, mb[:blk], mb[blk:]), oa[:, LANES:]

    dil2 = DILATED_PATTERNS[1][1]
    na = nseg // dil2
    cq = blk // na
    assert seg_len == 2 * blk and DILATED_PATTERNS[2][1] == nseg


    def p3_block(r, jb):
        rows = slice(jb * blk, (jb + 1) * blk)
        q01 = jnp.concatenate([qm_ref[0, r, rows, :], qm_ref[1, r, rows, :]], axis=0)
        acc, mx, den = attend(q01, k_ref[r], vm_ref[0, r], vm_ref[1, r], biast_ref[2, jb])
        acc_refs[1][r, rows, :] = acc
        m_refs[1][r, rows, :] = mx
        l_refs[1][r, rows, :] = den

    for r in range(nseg):
        prep(r)
        p3_block(r, 0)
        p3_block(r, 1)

    def p2_block(r4, j):
        iq, ik = j * cq, max(j - 1, 0) * cq
        segs = [dil2 * a + r4 for a in range(na)]
        q01 = jnp.concatenate([qm_ref[h, sg, iq:iq + cq, :] for h in range(2) for sg in segs], axis=0)
        kw = jnp.concatenate([k_ref[sg, ik:ik + 2 * cq, :] for sg in segs], axis=0)
        v0, v1 = [jnp.concatenate([vm_ref[h, sg, ik:ik + 2 * cq, :] for sg in segs], axis=0) for h in range(2)]
        acc, mx, den = attend(q01, kw, v0, v1, biast_ref[1, min(j, 1)])
        for a, sg in enumerate(segs):
            rows = slice(a * cq, (a + 1) * cq)
            acc_refs[0][sg, iq:iq + cq, :] = acc[rows]
            m_refs[0][sg, iq:iq + cq, :] = mx[rows]
            l_refs[0][sg, iq:iq + cq, :] = den[rows]

    for r4 in range(dil2):
        for j in range(seg_len // cq):
            p2_block(r4, j)

    def p1_block(j):
        iq, ik = j * SUBLANES, max(j - 1, 0) * SUBLANES

        def gather(ref, *lead, width=SUBLANES, start=iq):
            return jnp.concatenate([ref[lead + (r, slice(start, start + width), slice(None))] for r in range(nseg)],
                                   axis=0)

        q01 = jnp.concatenate([gather(qf_ref, 0), gather(qf_ref, 1)], axis=0).astype(BF16)
        kw = gather(kf_ref, width=2 * SUBLANES, start=ik).astype(BF16)
        v0 = gather(vf_ref, 0, width=2 * SUBLANES, start=ik).astype(BF16)
        v1 = gather(vf_ref, 1, width=2 * SUBLANES, start=ik).astype(BF16)
        stats = [attend(q01, kw, v0, v1, biast_ref[0, min(j, 1)])]
        stats += [(gather(acc_refs[p]), gather(m_refs[p]), gather(l_refs[p])) for p in range(2)]
        mtop = jnp.maximum(jnp.maximum(stats[0][1], stats[1][1]), stats[2][1])
        num = jnp.zeros((blk, LANES), F32)
        den = jnp.zeros((blk, LANES), F32)
        for acc_p, m_p, l_p in stats:
            e = jnp.exp2(m_p - mtop)
            num = num + e * acc_p
            den = den + e * l_p
        out = num / den * gather(gf_ref)
        t0 = j * blk
        for r in range(nseg):
            onat_ref[pl.ds(t0 + r, SUBLANES, stride=nseg), :] = out[r * SUBLANES:(r + 1) * SUBLANES]
        o_ref[t0:t0 + blk, :] = onat_ref[t0:t0 + blk, :].astype(BF16)

    for j in range(seg_len // SUBLANES):
        p1_block(j)


def _attn_masks():
    blk = ATTN_BLOCK
    tables = []
    for _, dil in DILATED_PATTERNS:
        na = NSEG // dil
        cq = blk // na
        qi = np.arange(blk)
        kj = np.arange(2 * blk)
        pos_q = na * (qi % cq) + qi // cq
        pos_k = na * (kj % (2 * cq)) + kj // (2 * cq)
        variants = []
        for q_off in (0, blk):
            dist = (pos_q[None, :] + q_off) - pos_k[:, None]
            variants.append(np.where((dist >= 0) & (dist <= blk), 0.0, MASK_VALUE))
        tables.append(np.stack(variants))
    return jnp.asarray(np.stack(tables).astype(np.float32), dtype=BF16)


def _attention(qkvg, seq):
    bsz, nseg, seg_len, _ = qkvg.shape
    n_pairs = D_ATTN // LANES
    blk = ATTN_BLOCK
    eye2 = jnp.asarray(np.tile(np.eye(blk, dtype=np.float32), (2, 1)), dtype=BF16)

    def col_spec(which):
        return pl.BlockSpec((None, nseg, seg_len, LANES), lambda b, hp: (b, 0, 0, which * n_pairs + hp))

    seg_f32 = pltpu.VMEM((nseg, seg_len, LANES), F32)
    pair_f32 = pltpu.VMEM((2, nseg, seg_len, LANES), F32)
    pair_bf16 = pltpu.VMEM((2, nseg, seg_len, LANES), BF16)
    scratch = [pair_f32, seg_f32, pair_f32, seg_f32, pair_bf16, pair_bf16, pltpu.VMEM((seq, LANES), F32)]
    scratch += [seg_f32] * 6
    return pl.pallas_call(
        _attn_kernel,
        grid=(bsz, n_pairs),
        in_specs=[col_spec(0), col_spec(1), col_spec(2), col_spec(3),
                  pl.BlockSpec((2 * blk, blk), lambda b, hp: (0, 0)),
                  pl.BlockSpec((len(DILATED_PATTERNS), 2, 2 * blk, blk), lambda b, hp: (0, 0, 0, 0))],
        out_specs=pl.BlockSpec((None, seq, LANES), lambda b, hp: (b, 0, hp)),
        out_shape=jax.ShapeDtypeStruct((bsz, seq, D_ATTN), BF16),
        scratch_shapes=scratch,
        compiler_params=pltpu.CompilerParams(
            dimension_semantics=("parallel", "parallel"), vmem_limit_bytes=VMEM_LIMIT),
        name="dilated_attn",
    )(qkvg, qkvg, qkvg, qkvg, eye2, _attn_masks())


SSD_T = 512


def _ssd_kernel(xbc_ref, z_ref, dt_ref, dtb_ref, alog_ref, dskip_ref,
                nw_ref, rexp_ref, tri_ref, y_ref, state_ref, ydiag_ref):
    @pl.when(pl.program_id(1) == 0)
    def _():
        state_ref[...] = jnp.zeros(state_ref.shape, F32)

    a_neg = -jnp.exp(alog_ref[...])
    li = lax.broadcasted_iota(jnp.int32, (CHUNK, CHUNK), 0)
    si = lax.broadcasted_iota(jnp.int32, (CHUNK, CHUNK), 1)
    causal = li >= si
    chan = lax.broadcasted_iota(jnp.int32, (CHUNK, D_SSM), 1)
    even_head = (chan // SSM_HEAD_DIM) % 2 == 0
    tri = tri_ref[...]
    rexp = rexp_ref[...]

    def expand(*es):
        parts = []
        for e in es:
            hi = e.astype(BF16)
            parts.append(jnp.concatenate([hi, (e - hi.astype(F32)).astype(BF16)], axis=1))
        out = jnp.dot(jnp.concatenate(parts, axis=0), rexp, preferred_element_type=F32)
        return [out[i * CHUNK:(i + 1) * CHUNK] for i in range(len(es))]

    for c in range(SSD_T // CHUNK):
        r0 = c * CHUNK
        xs = xbc_ref[r0:r0 + CHUNK, :D_SSM].astype(F32)

        dt_in = dt_ref[r0:r0 + CHUNK, :] + dtb_ref[...]
        dt = jnp.maximum(dt_in, 0.0) + jnp.log1p(jnp.exp(-jnp.abs(dt_in)))
        a_dt = dt * a_neg
        a_cs = sum(jnp.dot(tri, part, preferred_element_type=F32) for part in _split3(a_dt))
        a_cs_t = a_cs.T
        a_last = a_cs[CHUNK - 1:CHUNK, :]
        exp_acs_x, dt_x, dstate_x = expand(jnp.exp(a_cs), dt, jnp.exp(a_last - a_cs))
        cdecay_x = exp_acs_x[CHUNK - 1:CHUNK, :]

        xdt = xs * dt_x
        xdt_pair = (jnp.where(even_head, xdt, 0.0).astype(BF16), jnp.where(even_head, 0.0, xdt).astype(BF16))
        xdts_b = (xdt * dstate_x).astype(BF16)

        for g in range(SSM_GROUPS):
            gcols = slice(g * GROUP_W, (g + 1) * GROUP_W)
            b_b = xbc_ref[r0:r0 + CHUNK, D_SSM + g * D_STATE:D_SSM + (g + 1) * D_STATE]
            c_b = xbc_ref[r0:r0 + CHUNK, D_SSM + (SSM_GROUPS + g) * D_STATE:D_SSM + (SSM_GROUPS + g + 1) * D_STATE]
            b_f = b_b.astype(F32)
            cb = lax.dot_general(c_b, b_b, (((1,), (1,)), ((), ())), preferred_element_type=F32)
            s_prev = state_ref[g]
            y_off = jnp.dot(c_b, s_prev.astype(BF16), preferred_element_type=F32) * exp_acs_x[:, gcols]
            new_state = jnp.dot(b_f.T.astype(BF16), xdts_b[:, gcols], preferred_element_type=F32)
            state_ref[g] = s_prev * cdecay_x[:, gcols] + new_state
            for hp in range(HEADS_PER_GROUP // 2):
                pcols = slice(g * GROUP_W + hp * LANES, g * GROUP_W + (hp + 1) * LANES)
                mats = []
                for e in range(2):
                    h = g * HEADS_PER_GROUP + 2 * hp + e
                    seg = a_cs[:, h:h + 1] - a_cs_t[h:h + 1, :]
                    decay = jnp.exp(jnp.where(causal, seg, -jnp.inf))
                    mats.append((cb * decay).astype(BF16))
                ydiag_ref[:, pcols] = jnp.dot(jnp.concatenate(mats, axis=1),
                                              jnp.concatenate([xp[:, pcols] for xp in xdt_pair], axis=0),
                                              preferred_element_type=F32)
            y = ydiag_ref[:, gcols] + y_off + dskip_ref[:, gcols] * xs[:, gcols]
            yz = y * z_ref[r0:r0 + CHUNK, gcols].astype(F32)
            ms = jnp.mean(yz * yz, axis=-1, keepdims=True)
            y_ref[r0:r0 + CHUNK, gcols] = (yz * lax.rsqrt(ms + NORM_EPS) * nw_ref[:, gcols]).astype(BF16)


def _ssd(xbc3, z3, dt3, dt_bias, a_log, d_skip, ssm_norm_w):
    bsz, seq, _ = xbc3.shape
    pad = LANES - SSM_HEADS
    dtb = jnp.pad(dt_bias, (0, pad))[None, :]
    alog = jnp.pad(a_log, (0, pad))[None, :]
    dskip = jnp.repeat(d_skip, SSM_HEAD_DIM)[None, :]
    rexp = (np.arange(LANES)[:, None] == (np.arange(D_SSM)[None, :] // SSM_HEAD_DIM))
    rexp = jnp.asarray(np.tile(rexp.astype(np.float32), (2, 1)), dtype=BF16)
    tri = jnp.asarray(np.tril(np.ones((CHUNK, CHUNK), np.float32)), dtype=BF16)

    def const(shape):
        return pl.BlockSpec(shape, lambda b, i: (0,) * len(shape))

    return pl.pallas_call(
        _ssd_kernel,
        grid=(bsz, seq // SSD_T),
        in_specs=[
            pl.BlockSpec((None, SSD_T, D_CONV), lambda b, i: (b, i, 0)),
            pl.BlockSpec((None, SSD_T, D_SSM), lambda b, i: (b, i, 0)),
            pl.BlockSpec((None, SSD_T, LANES), lambda b, i: (b, i, 0)),
            const((1, LANES)), const((1, LANES)),
            const((1, D_SSM)), const((1, D_SSM)), const((2 * LANES, D_SSM)), const((CHUNK, CHUNK)),
        ],
        out_specs=pl.BlockSpec((None, SSD_T, D_SSM), lambda b, i: (b, i, 0)),
        out_shape=jax.ShapeDtypeStruct((bsz, seq, D_SSM), BF16),
        scratch_shapes=[
            pltpu.VMEM((SSM_GROUPS, D_STATE, GROUP_W), F32),
            pltpu.VMEM((CHUNK, D_SSM), F32),
        ],
        compiler_params=pltpu.CompilerParams(
            dimension_semantics=("parallel", "arbitrary"), vmem_limit_bytes=VMEM_LIMIT),
        name="ssd",
    )(xbc3, z3, dt3, dtb, alog, dskip, ssm_norm_w[None, :], rexp, tri)


OUT_TM = 512


def _outproj_kernel(attn_ref, y_ref, x_ref, w_ref, nw_ref, o_ref):
    out = jnp.dot(attn_ref[...], w_ref[:D_ATTN, :], preferred_element_type=F32)
    out = out + jnp.dot(y_ref[...], w_ref[D_ATTN:, :], preferred_element_type=F32)
    ms = jnp.mean(out * out, axis=-1, keepdims=True)
    o_ref[...] = x_ref[...] + out * lax.rsqrt(ms + NORM_EPS) * nw_ref[...]


def _outproj(attn2, y2, x2, w_out_b, norm_w):
    m = x2.shape[0]
    return pl.pallas_call(
        _outproj_kernel,
        grid=(m // OUT_TM,),
        in_specs=[
            pl.BlockSpec((OUT_TM, D_ATTN), lambda i: (i, 0)),
            pl.BlockSpec((OUT_TM, D_SSM), lambda i: (i, 0)),
            pl.BlockSpec((OUT_TM, D_MODEL), lambda i: (i, 0)),
            pl.BlockSpec((D_ATTN + D_SSM, D_MODEL), lambda i: (0, 0)),
            pl.BlockSpec((1, D_MODEL), lambda i: (0, 0)),
        ],
        out_specs=pl.BlockSpec((OUT_TM, D_MODEL), lambda i: (i, 0)),
        out_shape=jax.ShapeDtypeStruct((m, D_MODEL), F32),
        compiler_params=pltpu.CompilerParams(
            dimension_semantics=("parallel",), vmem_limit_bytes=VMEM_LIMIT),
        name="outproj",
    )(attn2, y2, x2, w_out_b, norm_w)


def _layer(hid, norm_pre_w, w_in, conv_w, conv_b, dt_bias, a_log, d_skip, ssm_norm_w, w_out, norm_post_w):
    bsz, seq, _ = hid.shape
    assert hid.shape[-1] == D_MODEL and w_in.shape == (D_MODEL, D_IN_PROJ)
    assert seq == 2 * ATTN_BLOCK * NSEG and seq % SSD_T == 0 and (bsz * seq) % OUT_TM == 0
    w_ssm = jnp.pad(w_in[:, D_QKVG:], ((0, 0), (0, D_SSM_PAD - (D_IN_PROJ - D_QKVG)))).astype(BF16)
    q_scale = ATTN_HEAD_DIM ** -0.5 * np.log2(np.e)
    col_scale = np.where(np.arange(D_QKVG) < D_ATTN, q_scale, 1.0).astype(np.float32)
    w_attn = (w_in[:, :D_QKVG] * col_scale).astype(BF16)
    qkvg, z, xbc, dt_raw = _inproj(hid, norm_pre_w[None, :], w_attn, w_ssm, conv_w, conv_b[None, :])
    attn = _attention(qkvg, seq)
    y = _ssd(xbc, z, dt_raw, dt_bias, a_log, d_skip, ssm_norm_w)
    x2 = hid.reshape(bsz * seq, D_MODEL)
    out = _outproj(attn.reshape(bsz * seq, D_ATTN), y.reshape(bsz * seq, D_SSM), x2,
                   w_out.astype(BF16), norm_post_w[None, :])
    return out.reshape(bsz, seq, D_MODEL)


def kernel(x, norm_pre_w, w_in, conv_w, conv_b, dt_bias, a_log, d_skip, ssm_norm_w, w_out, norm_post_w):
    hid = x
    for layer in range(norm_pre_w.shape[0]):
        hid = _layer(hid, norm_pre_w[layer], w_in[layer], conv_w[layer], conv_b[layer],
                     dt_bias[layer], a_log[layer], d_skip[layer], ssm_norm_w[layer],
                     w_out[layer], norm_post_w[layer])
    return hid
```

```python
import functools

import jax
import jax.numpy as jnp
import numpy as np
from jax import lax
from jax.experimental import pallas as pl
from jax.experimental.pallas import tpu as pltpu

D_MODEL = 1024
ATTN_HEADS = 16
ATTN_HEAD_DIM = 64
D_ATTN = ATTN_HEADS * ATTN_HEAD_DIM
DILATED_PATTERNS = ((128, 1), (512, 4), (2048, 16))
ATTN_BLOCK = 128
D_SSM = 1024
SSM_HEAD_DIM = 64
SSM_HEADS = D_SSM // SSM_HEAD_DIM
SSM_GROUPS = 2
D_STATE = 128
CONV_K = 4
CHUNK = 128
D_CONV = D_SSM + 2 * SSM_GROUPS * D_STATE
D_QKVG = 4 * D_ATTN
D_IN_PROJ = D_QKVG + D_SSM + D_CONV + SSM_HEADS
NORM_EPS = 1e-6

LANES = 128
SUBLANES = 8
D_SSM_PAD = D_SSM + D_CONV + LANES
GROUP_W = D_SSM // SSM_GROUPS
HEADS_PER_GROUP = SSM_HEADS // SSM_GROUPS
MASK_VALUE = -1e30
VMEM_LIMIT = 56 * 1024 * 1024

NSEG = max(d for _, d in DILATED_PATTERNS)

F32 = jnp.float32
BF16 = jnp.bfloat16


def _silu(x):
    h = 0.5 * x
    return h + h * jnp.tanh(h)


def _split3(x):
    hi = x.astype(BF16)
    r1 = x - hi.astype(F32)
    mid = r1.astype(BF16)
    lo = (r1 - mid.astype(F32)).astype(BF16)
    return hi, mid, lo


IN_CHUNK = 512
IN_TM = 512
IN_ROWS = IN_TM // NSEG


def _rms_normed(xf, w):
    ms = jnp.mean(xf * xf, axis=-1, keepdims=True)
    return xf * lax.rsqrt(ms + NORM_EPS) * w


def _inproj_kernel(x_ref, nw_ref, wa_ref, qkvg_ref, slab_ref, ur_ref):
    uf = _rms_normed(x_ref[...], nw_ref[...])

    n_slabs = D_MODEL // LANES
    for c in range(n_slabs):
        slab_ref[c] = uf[:, c * LANES:(c + 1) * LANES]
    for c in range(n_slabs):
        for r in range(NSEG):
            ur_ref[r * IN_ROWS:(r + 1) * IN_ROWS, c * LANES:(c + 1) * LANES] = (
                slab_ref[c, pl.ds(r, IN_ROWS, stride=NSEG), :].astype(BF16))

    for c0 in range(0, D_QKVG, IN_CHUNK):
        res = jnp.dot(ur_ref[...], wa_ref[:, c0:c0 + IN_CHUNK], preferred_element_type=F32)
        if c0 >= 3 * D_ATTN:
            res = _silu(res)
        res = res.astype(BF16)
        for r in range(NSEG):
            qkvg_ref[r, :, c0:c0 + IN_CHUNK] = res[r * IN_ROWS:(r + 1) * IN_ROWS, :]


def _inproj(x, norm_w, w_attn):
    bsz, seq, _ = x.shape
    seg_len = seq // NSEG
    return pl.pallas_call(
        _inproj_kernel,
        grid=(bsz, seq // IN_TM),
        in_specs=[
            pl.BlockSpec((None, IN_TM, D_MODEL), lambda b, k: (b, k, 0)),
            pl.BlockSpec((1, D_MODEL), lambda b, k: (0, 0)),
            pl.BlockSpec((D_MODEL, D_QKVG), lambda b, k: (0, 0), pipeline_mode=pl.Buffered(1)),
        ],
        out_specs=pl.BlockSpec((None, NSEG, IN_ROWS, D_QKVG), lambda b, k: (b, 0, k, 0)),
        out_shape=jax.ShapeDtypeStruct((bsz, NSEG, seg_len, D_QKVG), BF16),
        scratch_shapes=[
            pltpu.VMEM((D_MODEL // LANES, IN_TM, LANES), F32),
            pltpu.VMEM((IN_TM, D_MODEL), BF16),
        ],
        compiler_params=pltpu.CompilerParams(
            dimension_semantics=("parallel", "parallel"), vmem_limit_bytes=VMEM_LIMIT),
        name="inproj",
    )(x, norm_w, w_attn)


def _attn_kernel(q_ref, k_ref, v_ref, g_ref, eye_ref, biast_ref, o_ref,
                 qf_ref, kf_ref, vf_ref, gf_ref, qm_ref, vm_ref, onat_ref, *stat_refs):
    acc_refs, m_refs, l_refs = stat_refs[0:2], stat_refs[2:4], stat_refs[4:6]
    nseg, seg_len, _ = q_ref.shape
    blk = ATTN_BLOCK
    head0 = lax.broadcasted_iota(jnp.int32, (blk, LANES), 1) < ATTN_HEAD_DIM
    head0_seg = lax.broadcasted_iota(jnp.int32, (seg_len, LANES), 1) < ATTN_HEAD_DIM
    head0_win = lax.broadcasted_iota(jnp.int32, (2 * blk, LANES), 1) < ATTN_HEAD_DIM
    ones_h = (jnp.where(head0_win, 1.0, 0.0).astype(BF16), jnp.where(head0_win, 0.0, 1.0).astype(BF16))

    def prep(r):
        gf_ref[r] = g_ref[r].astype(F32)
        qf = q_ref[r].astype(F32)
        vf = v_ref[r].astype(F32)
        for h in range(2):
            keep = head0_seg if h == 0 else jnp.logical_not(head0_seg)
            qh = jnp.where(keep, qf, 0.0)
            vh = jnp.where(keep, vf, 0.0)
            qf_ref[h, r] = qh
            vf_ref[h, r] = vh
            qm_ref[h, r] = qh.astype(BF16)
            vm_ref[h, r] = vh.astype(BF16)
        kf_ref[r] = k_ref[r].astype(F32)

    def attend(q01, kw, v0, v1, bias_t):
        qa = jnp.concatenate([q01, eye_ref[...]], axis=1)
        ka = jnp.concatenate([kw, bias_t], axis=1)
        s = lax.dot_general(qa, ka, (((1,), (1,)), ((), ())), preferred_element_type=F32)
        m = jnp.max(s, axis=-1, keepdims=True)
        pr = jnp.exp2(s - m).astype(BF16)
        lhs = jnp.concatenate([pr[:blk], pr[blk:]], axis=1)
        rhs = jnp.concatenate([jnp.concatenate([v0, ones_h[0]], axis=1),
                               jnp.concatenate([v1, ones_h[1]], axis=1)], axis=0)
        oa = jnp.dot(lhs, rhs, preferred_element_type=F32)
        mb = jnp.broadcast_to(m, (2 * blk, LANES))
        return oa[:, :LANES], jnp.where(head0, mb[:blk], mb[blk:]), oa[:, LANES:]

    dil2 = DILATED_PATTERNS[1][1]
    na = nseg // dil2
    cq = blk // na
    assert seg_len == 2 * blk and DILATED_PATTERNS[2][1] == nseg


    def p3_block(r, jb):
        rows = slice(jb * blk, (jb + 1) * blk)
        q01 = jnp.concatenate([qm_ref[0, r, rows, :], qm_ref[1, r, rows, :]], axis=0)
        acc, mx, den = attend(q01, k_ref[r], vm_ref[0, r], vm_ref[1, r], biast_ref[2, jb])
        acc_refs[1][r, rows, :] = acc
        m_refs[1][r, rows, :] = mx
        l_refs[1][r, rows, :] = den

    for r in range(nseg):
        prep(r)
        p3_block(r, 0)
        p3_block(r, 1)

    def p2_block(r4, j):
        iq, ik = j * cq, max(j - 1, 0) * cq
        segs = [dil2 * a + r4 for a in range(na)]
        q01 = jnp.concatenate([qm_ref[h, sg, iq:iq + cq, :] for h in range(2) for sg in segs], axis=0)
        kw = jnp.concatenate([k_ref[sg, ik:ik + 2 * cq, :] for sg in segs], axis=0)
        v0, v1 = [jnp.concatenate([vm_ref[h, sg, ik:ik + 2 * cq, :] for sg in segs], axis=0) for h in range(2)]
        acc, mx, den = attend(q01, kw, v0, v1, biast_ref[1, min(j, 1)])
        for a, sg in enumerate(segs):
            rows = slice(a * cq, (a + 1) * cq)
            acc_refs[0][sg, iq:iq + cq, :] = acc[rows]
            m_refs[0][sg, iq:iq + cq, :] = mx[rows]
            l_refs[0][sg, iq:iq + cq, :] = den[rows]

    for r4 in range(dil2):
        for j in range(seg_len // cq):
            p2_block(r4, j)

    def p1_block(j):
        iq, ik = j * SUBLANES, max(j - 1, 0) * SUBLANES

        def gather(ref, *lead, width=SUBLANES, start=iq):
            return jnp.concatenate([ref[lead + (r, slice(start, start + width), slice(None))] for r in range(nseg)],
                                   axis=0)

        q01 = jnp.concatenate([gather(qf_ref, 0), gather(qf_ref, 1)], axis=0).astype(BF16)
        kw = gather(kf_ref, width=2 * SUBLANES, start=ik).astype(BF16)
        v0 = gather(vf_ref, 0, width=2 * SUBLANES, start=ik).astype(BF16)
        v1 = gather(vf_ref, 1, width=2 * SUBLANES, start=ik).astype(BF16)
        stats = [attend(q01, kw, v0, v1, biast_ref[0, min(j, 1)])]
        stats += [(gather(acc_refs[p]), gather(m_refs[p]), gather(l_refs[p])) for p in range(2)]
        mtop = jnp.maximum(jnp.maximum(stats[0][1], stats[1][1]), stats[2][1])
        num = jnp.zeros((blk, LANES), F32)
        den = jnp.zeros((blk, LANES), F32)
        for acc_p, m_p, l_p in stats:
            e = jnp.exp2(m_p - mtop)
            num = num + e * acc_p
            den = den + e * l_p
        out = num / den * gather(gf_ref)
        t0 = j * blk
        for r in range(nseg):
            onat_ref[pl.ds(t0 + r, SUBLANES, stride=nseg), :] = out[r * SUBLANES:(r + 1) * SUBLANES]
        o_ref[t0:t0 + blk, :] = onat_ref[t0:t0 + blk, :].astype(BF16)

    for j in range(seg_len // SUBLANES):
        p1_block(j)


def _attn_masks():
    blk = ATTN_BLOCK
    tables = []
    for _, dil in DILATED_PATTERNS:
        na = NSEG // dil
        cq = blk // na
        qi = np.arange(blk)
        kj = np.arange(2 * blk)
        pos_q = na * (qi % cq) + qi // cq
        pos_k = na * (kj % (2 * cq)) + kj // (2 * cq)
        variants = []
        for q_off in (0, blk):
            dist = (pos_q[None, :] + q_off) - pos_k[:, None]
            variants.append(np.where((dist >= 0) & (dist <= blk), 0.0, MASK_VALUE))
        tables.append(np.stack(variants))
    return jnp.asarray(np.stack(tables).astype(np.float32), dtype=BF16)


def _attention(qkvg, seq):
    bsz, nseg, seg_len, _ = qkvg.shape
    n_pairs = D_ATTN // LANES
    blk = ATTN_BLOCK
    eye2 = jnp.asarray(np.tile(np.eye(blk, dtype=np.float32), (2, 1)), dtype=BF16)

    def col_spec(which):
        return pl.BlockSpec((None, nseg, seg_len, LANES), lambda b, hp: (b, 0, 0, which * n_pairs + hp))

    seg_f32 = pltpu.VMEM((nseg, seg_len, LANES), F32)
    pair_f32 = pltpu.VMEM((2, nseg, seg_len, LANES), F32)
    pair_bf16 = pltpu.VMEM((2, nseg, seg_len, LANES), BF16)
    scratch = [pair_f32, seg_f32, pair_f32, seg_f32, pair_bf16, pair_bf16, pltpu.VMEM((seq, LANES), F32)]
    scratch += [seg_f32] * 6
    return pl.pallas_call(
        _attn_kernel,
        grid=(bsz, n_pairs),
        in_specs=[col_spec(0), col_spec(1), col_spec(2), col_spec(3),
                  pl.BlockSpec((2 * blk, blk), lambda b, hp: (0, 0)),
                  pl.BlockSpec((len(DILATED_PATTERNS), 2, 2 * blk, blk), lambda b, hp: (0, 0, 0, 0))],
        out_specs=pl.BlockSpec((None, seq, LANES), lambda b, hp: (b, 0, hp)),
        out_shape=jax.ShapeDtypeStruct((bsz, seq, D_ATTN), BF16),
        scratch_shapes=scratch,
        compiler_params=pltpu.CompilerParams(
            dimension_semantics=("parallel", "parallel"), vmem_limit_bytes=VMEM_LIMIT),
        name="dilated_attn",
    )(qkvg, qkvg, qkvg, qkvg, eye2, _attn_masks())


SSD_T = 512
SSD_PREV = 16
SSD_CHUNK = 512


def _ssd_kernel(x_ref, xprev_ref, nwin_ref, ws_ref, cw_ref, cb_ref, dtb_ref, alog_ref, dskip_ref,
                nw_ref, rexp_ref, tri_ref, y_ref, xbc_ref, z_ref, dt_ref, cx_ref, ilv_ref, state_ref, ydiag_ref):
    step = pl.program_id(1)

    @pl.when(step == 0)
    def _():
        state_ref[...] = jnp.zeros(state_ref.shape, F32)

    un = _rms_normed(x_ref[...], nwin_ref[...]).astype(BF16)
    up = _rms_normed(xprev_ref[...], nwin_ref[...]).astype(BF16)
    keep_prev = (step > 0).astype(F32)

    def proj(c0, width):
        return jnp.dot(un, ws_ref[:, c0:c0 + width], preferred_element_type=F32)

    def front_conv(ci):
        c0 = ci * SSD_CHUNK
        wcols = slice(D_SSM + c0, D_SSM + c0 + SSD_CHUNK)
        prev = jnp.dot(up, ws_ref[:, wcols], preferred_element_type=F32) * keep_prev
        cur = proj(D_SSM + c0, SSD_CHUNK)
        half = SSD_T // 2
        for a in range(SSD_CHUNK // LANES):
            sl = ci * (SSD_CHUNK // LANES) + a
            lanes = slice(a * LANES, (a + 1) * LANES)
            cols = slice(c0 + a * LANES, c0 + (a + 1) * LANES)
            cx_ref[sl, 0:SSD_PREV, :] = prev[:, lanes]
            cx_ref[sl, SSD_PREV:, :] = cur[:, lanes]
            first = SSD_PREV - (CONV_K - 1)
            rows2 = [cx_ref[sl, pl.ds(first + k, half, stride=2), :] for k in range(CONV_K + 1)]
            for parity in range(2):
                conv = cb_ref[:, cols]
                for k in range(CONV_K):
                    conv = conv + cw_ref[k:k + 1, cols] * rows2[k + parity]
                ilv_ref[sl, pl.ds(parity, half, stride=2), :] = _silu(conv)
            xbc_ref[:, cols] = ilv_ref[sl].astype(BF16)

    def front_gate():
        dt_ref[...] = proj(D_SSM + D_CONV, LANES)
        for c0 in range(0, D_SSM, SSD_CHUNK):
            z_ref[:, c0:c0 + SSD_CHUNK] = _silu(proj(c0, SSD_CHUNK)).astype(BF16)

    a_neg = -jnp.exp(alog_ref[...])
    li = lax.broadcasted_iota(jnp.int32, (CHUNK, CHUNK), 0)
    si = lax.broadcasted_iota(jnp.int32, (CHUNK, CHUNK), 1)
    causal = li >= si
    chan = lax.broadcasted_iota(jnp.int32, (CHUNK, D_SSM), 1)
    even_head = (chan // SSM_HEAD_DIM) % 2 == 0
    tri = tri_ref[...]
    rexp = rexp_ref[...]

    def expand(*es):
        parts = []
        for e in es:
            hi = e.astype(BF16)
            parts.append(jnp.concatenate([hi, (e - hi.astype(F32)).astype(BF16)], axis=1))
        out = jnp.dot(jnp.concatenate(parts, axis=0), rexp, preferred_element_type=F32)
        return [out[i * CHUNK:(i + 1) * CHUNK] for i in range(len(es))]

    def scan_chunk(c):
        r0 = c * CHUNK
        xs = xbc_ref[r0:r0 + CHUNK, :D_SSM].astype(F32)

        dt_in = dt_ref[r0:r0 + CHUNK, :] + dtb_ref[...]
        dt = jnp.maximum(dt_in, 0.0) + jnp.log(1.0 + jnp.exp(-jnp.abs(dt_in)))
        a_dt = dt * a_neg
        a_cs = sum(jnp.dot(tri, part, preferred_element_type=F32) for part in _split3(a_dt))
        a_cs_t = a_cs.T
        a_last = a_cs[CHUNK - 1:CHUNK, :]
        exp_acs_x, dt_x, dstate_x = expand(jnp.exp(a_cs), dt, jnp.exp(a_last - a_cs))
        cdecay_x = exp_acs_x[CHUNK - 1:CHUNK, :]

        xdt = xs * dt_x
        xdt_pair = (jnp.where(even_head, xdt, 0.0).astype(BF16), jnp.where(even_head, 0.0, xdt).astype(BF16))
        xdts_b = (xdt * dstate_x).astype(BF16)

        for g in range(SSM_GROUPS):
            gcols = slice(g * GROUP_W, (g + 1) * GROUP_W)
            b_b = xbc_ref[r0:r0 + CHUNK, D_SSM + g * D_STATE:D_SSM + (g + 1) * D_STATE]
            c_b = xbc_ref[r0:r0 + CHUNK, D_SSM + (SSM_GROUPS + g) * D_STATE:D_SSM + (SSM_GROUPS + g + 1) * D_STATE]
            b_f = b_b.astype(F32)
            cb = lax.dot_general(c_b, b_b, (((1,), (1,)), ((), ())), preferred_element_type=F32)
            s_prev = state_ref[g]
            y_off = jnp.dot(c_b, s_prev.astype(BF16), preferred_element_type=F32) * exp_acs_x[:, gcols]
            new_state = jnp.dot(b_f.T.astype(BF16), xdts_b[:, gcols], preferred_element_type=F32)
            state_ref[g] = s_prev * cdecay_x[:, gcols] + new_state
            for hp in range(HEADS_PER_GROUP // 2):
                pcols = slice(g * GROUP_W + hp * LANES, g * GROUP_W + (hp + 1) * LANES)
                mats = []
                for e in range(2):
                    h = g * HEADS_PER_GROUP + 2 * hp + e
                    seg = a_cs[:, h:h + 1] - a_cs_t[h:h + 1, :]
                    decay = jnp.exp(jnp.where(causal, seg, -jnp.inf))
                    mats.append((cb * decay).astype(BF16))
                ydiag_ref[:, pcols] = jnp.dot(jnp.concatenate(mats, axis=1),
                                              jnp.concatenate([xp[:, pcols] for xp in xdt_pair], axis=0),
                                              preferred_element_type=F32)
            y = ydiag_ref[:, gcols] + y_off + dskip_ref[:, gcols] * xs[:, gcols]
            yz = y * z_ref[r0:r0 + CHUNK, gcols].astype(F32)
            ms = jnp.mean(yz * yz, axis=-1, keepdims=True)
            y_ref[r0:r0 + CHUNK, gcols] = (yz * lax.rsqrt(ms + NORM_EPS) * nw_ref[:, gcols]).astype(BF16)

    for ci in range(D_CONV // SSD_CHUNK):
        front_conv(ci)
    front_gate()
    for c in range(SSD_T // CHUNK):
        scan_chunk(c)


def _ssd(x, norm_w, w_ssm, conv_w, conv_b, dt_bias, a_log, d_skip, ssm_norm_w):
    bsz, seq, _ = x.shape
    prev_per_tile = SSD_T // SSD_PREV
    pad = LANES - SSM_HEADS
    dtb = jnp.pad(dt_bias, (0, pad))[None, :]
    alog = jnp.pad(a_log, (0, pad))[None, :]
    dskip = jnp.repeat(d_skip, SSM_HEAD_DIM)[None, :]
    rexp = (np.arange(LANES)[:, None] == (np.arange(D_SSM)[None, :] // SSM_HEAD_DIM))
    rexp = jnp.asarray(np.tile(rexp.astype(np.float32), (2, 1)), dtype=BF16)
    tri = jnp.asarray(np.tril(np.ones((CHUNK, CHUNK), np.float32)), dtype=BF16)

    def const(shape):
        return pl.BlockSpec(shape, lambda b, i: (0,) * len(shape))

    return pl.pallas_call(
        _ssd_kernel,
        grid=(bsz, seq // SSD_T),
        in_specs=[
            pl.BlockSpec((None, SSD_T, D_MODEL), lambda b, i: (b, i, 0)),
            pl.BlockSpec((None, SSD_PREV, D_MODEL), lambda b, i: (b, jnp.maximum(i * prev_per_tile - 1, 0), 0)),
            const((1, D_MODEL)),
            pl.BlockSpec((D_MODEL, D_SSM_PAD), lambda b, i: (0, 0), pipeline_mode=pl.Buffered(1)),
            const((CONV_K, D_CONV)), const((1, D_CONV)),
            const((1, LANES)), const((1, LANES)),
            const((1, D_SSM)), const((1, D_SSM)), const((2 * LANES, D_SSM)), const((CHUNK, CHUNK)),
        ],
        out_specs=pl.BlockSpec((None, SSD_T, D_SSM), lambda b, i: (b, i, 0)),
        out_shape=jax.ShapeDtypeStruct((bsz, seq, D_SSM), BF16),
        scratch_shapes=[
            pltpu.VMEM((SSD_T, D_CONV), BF16),
            pltpu.VMEM((SSD_T, D_SSM), BF16),
            pltpu.VMEM((SSD_T, LANES), F32),
            pltpu.VMEM((D_CONV // LANES, SSD_PREV + SSD_T, LANES), F32),
            pltpu.VMEM((D_CONV // LANES, SSD_T, LANES), F32),
            pltpu.VMEM((SSM_GROUPS, D_STATE, GROUP_W), F32),
            pltpu.VMEM((CHUNK, D_SSM), F32),
        ],
        compiler_params=pltpu.CompilerParams(
            dimension_semantics=("parallel", "arbitrary"), vmem_limit_bytes=VMEM_LIMIT),
        name="ssd",
    )(x, x, norm_w, w_ssm, conv_w, conv_b, dtb, alog, dskip, ssm_norm_w[None, :], rexp, tri)


OUT_TM = 1024


def _outproj_kernel(attn_ref, y_ref, x_ref, w_ref, nw_ref, o_ref):
    out = jnp.dot(attn_ref[...], w_ref[:D_ATTN, :], preferred_element_type=F32)
    out = out + jnp.dot(y_ref[...], w_ref[D_ATTN:, :], preferred_element_type=F32)
    ms = jnp.mean(out * out, axis=-1, keepdims=True)
    o_ref[...] = x_ref[...] + out * lax.rsqrt(ms + NORM_EPS) * nw_ref[...]


def _outproj(attn2, y2, x2, w_out_b, norm_w):
    m = x2.shape[0]
    return pl.pallas_call(
        _outproj_kernel,
        grid=(m // OUT_TM,),
        in_specs=[
            pl.BlockSpec((OUT_TM, D_ATTN), lambda i: (i, 0)),
            pl.BlockSpec((OUT_TM, D_SSM), lambda i: (i, 0)),
            pl.BlockSpec((OUT_TM, D_MODEL), lambda i: (i, 0)),
            pl.BlockSpec((D_ATTN + D_SSM, D_MODEL), lambda i: (0, 0)),
            pl.BlockSpec((1, D_MODEL), lambda i: (0, 0)),
        ],
        out_specs=pl.BlockSpec((OUT_TM, D_MODEL), lambda i: (i, 0)),
        out_shape=jax.ShapeDtypeStruct((m, D_MODEL), F32),
        compiler_params=pltpu.CompilerParams(
            dimension_semantics=("parallel",), vmem_limit_bytes=VMEM_LIMIT),
        name="outproj",
    )(attn2, y2, x2, w_out_b, norm_w)


def _layer(hid, norm_pre_w, w_in, conv_w, conv_b, dt_bias, a_log, d_skip, ssm_norm_w, w_out, norm_post_w):
    bsz, seq, _ = hid.shape
    assert hid.shape[-1] == D_MODEL and w_in.shape == (D_MODEL, D_IN_PROJ)
    assert seq == 2 * ATTN_BLOCK * NSEG and seq % SSD_T == 0 and (bsz * seq) % OUT_TM == 0
    w_ssm = jnp.pad(w_in[:, D_QKVG:], ((0, 0), (0, D_SSM_PAD - (D_IN_PROJ - D_QKVG)))).astype(BF16)
    q_scale = ATTN_HEAD_DIM ** -0.5 * np.log2(np.e)
    col_scale = np.where(np.arange(D_QKVG) < D_ATTN, q_scale, 1.0).astype(np.float32)
    w_attn = (w_in[:, :D_QKVG] * col_scale).astype(BF16)
    qkvg = _inproj(hid, norm_pre_w[None, :], w_attn)
    attn = _attention(qkvg, seq)
    y = _ssd(hid, norm_pre_w[None, :], w_ssm, conv_w, conv_b[None, :], dt_bias, a_log, d_skip, ssm_norm_w)
    x2 = hid.reshape(bsz * seq, D_MODEL)
    out = _outproj(attn.reshape(bsz * seq, D_ATTN), y.reshape(bsz * seq, D_SSM), x2,
                   w_out.astype(BF16), norm_post_w[None, :])
    return out.reshape(bsz, seq, D_MODEL)


def kernel(x, norm_pre_w, w_in, conv_w, conv_b, dt_bias, a_log, d_skip, ssm_norm_w, w_out, norm_post_w):
    hid = x
    for layer in range(norm_pre_w.shape[0]):
        hid = _layer(hid, norm_pre_w[layer], w_in[layer], conv_w[layer], conv_b[layer],
                     dt_bias[layer], a_log[layer], d_skip[layer], ssm_norm_w[layer],
                     w_out[layer], norm_post_w[layer])
    return hid
```

```python
import jax
import jax.numpy as jnp
import numpy as np
from jax import lax
from jax.experimental import pallas as pl
from jax.experimental.pallas import tpu as pltpu

D_MODEL = 1024
ATTN_HEADS = 16
ATTN_HEAD_DIM = 64
D_ATTN = ATTN_HEADS * ATTN_HEAD_DIM
DILATED_PATTERNS = ((128, 1), (512, 4), (2048, 16))
ATTN_BLOCK = 128
D_SSM = 1024
SSM_HEAD_DIM = 64
SSM_HEADS = D_SSM // SSM_HEAD_DIM
SSM_GROUPS = 2
D_STATE = 128
CONV_K = 4
CHUNK = 128
D_CONV = D_SSM + 2 * SSM_GROUPS * D_STATE
D_QKVG = 4 * D_ATTN
D_IN_PROJ = D_QKVG + D_SSM + D_CONV + SSM_HEADS
NORM_EPS = 1e-6

LANES = 128
SUBLANES = 8
D_SSM_PAD = D_SSM + D_CONV + LANES
GROUP_W = D_SSM // SSM_GROUPS
HEADS_PER_GROUP = SSM_HEADS // SSM_GROUPS
MASK_VALUE = -1e30
LOG2E = float(np.log2(np.e))
LN2 = float(np.log(2.0))
VMEM_LIMIT = 56 * 1024 * 1024

NSEG = max(d for _, d in DILATED_PATTERNS)

F32 = jnp.float32
BF16 = jnp.bfloat16


def _silu(x):
    h = 0.5 * x
    return h + h * jnp.tanh(h)


def _split3(x):
    hi = x.astype(BF16)
    r1 = x - hi.astype(F32)
    mid = r1.astype(BF16)
    lo = (r1 - mid.astype(F32)).astype(BF16)
    return hi, mid, lo


def _rms_normed(xf, w):
    ms = jnp.mean(xf * xf, axis=-1, keepdims=True)
    return xf * lax.rsqrt(ms + NORM_EPS) * w


IN_CHUNK = 512
IN_TM = 512
IN_ROWS = IN_TM // NSEG


def _inproj_kernel(x_ref, nw_ref, wa_ref, qkvg_ref, slab_ref, ur_ref):
    uf = _rms_normed(x_ref[...], nw_ref[...])

    n_slabs = D_MODEL // LANES
    for c in range(n_slabs):
        slab_ref[c] = uf[:, c * LANES:(c + 1) * LANES]
    for c in range(n_slabs):
        for r in range(NSEG):
            ur_ref[r * IN_ROWS:(r + 1) * IN_ROWS, c * LANES:(c + 1) * LANES] = (
                slab_ref[c, pl.ds(r, IN_ROWS, stride=NSEG), :].astype(BF16))

    for c0 in reversed(range(0, D_QKVG, IN_CHUNK)):
        res = jnp.dot(ur_ref[...], wa_ref[:, c0:c0 + IN_CHUNK], preferred_element_type=F32)
        if c0 >= 3 * D_ATTN:
            res = _silu(res)
        res = res.astype(BF16)
        for r in range(NSEG):
            qkvg_ref[r, :, c0:c0 + IN_CHUNK] = res[r * IN_ROWS:(r + 1) * IN_ROWS, :]


def _inproj(x, norm_w, w_attn):
    bsz, seq, _ = x.shape
    seg_len = seq // NSEG
    return pl.pallas_call(
        _inproj_kernel,
        grid=(bsz, seq // IN_TM),
        in_specs=[
            pl.BlockSpec((None, IN_TM, D_MODEL), lambda b, k: (b, k, 0)),
            pl.BlockSpec((1, D_MODEL), lambda b, k: (0, 0)),
            pl.BlockSpec((D_MODEL, D_QKVG), lambda b, k: (0, 0), pipeline_mode=pl.Buffered(1)),
        ],
        out_specs=pl.BlockSpec((None, NSEG, IN_ROWS, D_QKVG), lambda b, k: (b, 0, k, 0)),
        out_shape=jax.ShapeDtypeStruct((bsz, NSEG, seg_len, D_QKVG), BF16),
        scratch_shapes=[
            pltpu.VMEM((D_MODEL // LANES, IN_TM, LANES), F32),
            pltpu.VMEM((IN_TM, D_MODEL), BF16),
        ],
        compiler_params=pltpu.CompilerParams(
            dimension_semantics=("parallel", "parallel"), vmem_limit_bytes=VMEM_LIMIT),
        name="inproj",
    )(x, norm_w, w_attn)


def _attn_kernel(q_ref, k_ref, v_ref, g_ref, eye_ref, biast_ref, o_ref,
                 qf_ref, kf_ref, vf_ref, gf_ref, qm_ref, vm_ref, onat_ref, *stat_refs):
    acc_refs, m_refs, l_refs = stat_refs[0:2], stat_refs[2:4], stat_refs[4:6]
    nseg, seg_len, _ = q_ref.shape
    blk = ATTN_BLOCK
    head0 = lax.broadcasted_iota(jnp.int32, (blk, LANES), 1) < ATTN_HEAD_DIM
    head0_seg = lax.broadcasted_iota(jnp.int32, (seg_len, LANES), 1) < ATTN_HEAD_DIM
    head0_win = lax.broadcasted_iota(jnp.int32, (2 * blk, LANES), 1) < ATTN_HEAD_DIM
    ones_h = (jnp.where(head0_win, 1.0, 0.0).astype(BF16), jnp.where(head0_win, 0.0, 1.0).astype(BF16))

    def prep(r):
        gf_ref[r] = g_ref[r].astype(F32)
        qf = q_ref[r].astype(F32)
        vf = v_ref[r].astype(F32)
        for h in range(2):
            keep = head0_seg if h == 0 else jnp.logical_not(head0_seg)
            qh = jnp.where(keep, qf, 0.0)
            vh = jnp.where(keep, vf, 0.0)
            qf_ref[h, r] = qh
            vf_ref[h, r] = vh
            qm_ref[h, r] = qh.astype(BF16)
            vm_ref[h, r] = vh.astype(BF16)
        kf_ref[r] = k_ref[r].astype(F32)

    def attend(q01, kw, v0, v1, bias_t):
        qa = jnp.concatenate([q01, eye_ref[...]], axis=1)
        ka = jnp.concatenate([kw, bias_t], axis=1)
        s = lax.dot_general(qa, ka, (((1,), (1,)), ((), ())), preferred_element_type=F32)
        m = jnp.max(s, axis=-1, keepdims=True)
        pr = jnp.exp2(s - m).astype(BF16)
        lhs = jnp.concatenate([pr[:blk], pr[blk:]], axis=1)
        rhs = jnp.concatenate([jnp.concatenate([v0, ones_h[0]], axis=1),
                               jnp.concatenate([v1, ones_h[1]], axis=1)], axis=0)
        oa = jnp.dot(lhs, rhs, preferred_element_type=F32)
        mb = jnp.broadcast_to(m, (2 * blk, LANES))
        return oa[:, :LANES], jnp.where(head0, mb[:blk], mb[blk:]), oa[:, LANES:]

    dil2 = DILATED_PATTERNS[1][1]
    na = nseg // dil2
    cq = blk // na
    assert seg_len == 2 * blk and DILATED_PATTERNS[2][1] == nseg


    def p3_block(r, jb):
        rows = slice(jb * blk, (jb + 1) * blk)
        q01 = jnp.concatenate([qm_ref[0, r, rows, :], qm_ref[1, r, rows, :]], axis=0)
        acc, mx, den = attend(q01, k_ref[r], vm_ref[0, r], vm_ref[1, r], biast_ref[2, jb])
        acc_refs[1][r, rows, :] = acc
        m_refs[1][r, rows, :] = mx
        l_refs[1][r, rows, :] = den

    for r in range(nseg):
        prep(r)
        p3_block(r, 0)
        p3_block(r, 1)

    def p2_block(r4, j):
        iq, ik = j * cq, max(j - 1, 0) * cq
        segs = [dil2 * a + r4 for a in range(na)]
        q01 = jnp.concatenate([qm_ref[h, sg, iq:iq + cq, :] for h in range(2) for sg in segs], axis=0)
        kw = jnp.concatenate([k_ref[sg, ik:ik + 2 * cq, :] for sg in segs], axis=0)
        v0, v1 = [jnp.concatenate([vm_ref[h, sg, ik:ik + 2 * cq, :] for sg in segs], axis=0) for h in range(2)]
        acc, mx, den = attend(q01, kw, v0, v1, biast_ref[1, min(j, 1)])
        for a, sg in enumerate(segs):
            rows = slice(a * cq, (a + 1) * cq)
            acc_refs[0][sg, iq:iq + cq, :] = acc[rows]
            m_refs[0][sg, iq:iq + cq, :] = mx[rows]
            l_refs[0][sg, iq:iq + cq, :] = den[rows]

    for r4 in range(dil2):
        for j in range(seg_len // cq):
            p2_block(r4, j)

    def p1_block(j):
        iq, ik = j * SUBLANES, max(j - 1, 0) * SUBLANES

        def gather(ref, *lead, width=SUBLANES, start=iq):
            return jnp.concatenate([ref[lead + (r, slice(start, start + width), slice(None))] for r in range(nseg)],
                                   axis=0)

        q01 = jnp.concatenate([gather(qf_ref, 0), gather(qf_ref, 1)], axis=0).astype(BF16)
        kw = gather(kf_ref, width=2 * SUBLANES, start=ik).astype(BF16)
        v0 = gather(vf_ref, 0, width=2 * SUBLANES, start=ik).astype(BF16)
        v1 = gather(vf_ref, 1, width=2 * SUBLANES, start=ik).astype(BF16)
        stats = [attend(q01, kw, v0, v1, biast_ref[0, min(j, 1)])]
        stats += [(gather(acc_refs[p]), gather(m_refs[p]), gather(l_refs[p])) for p in range(2)]
        mtop = jnp.maximum(jnp.maximum(stats[0][1], stats[1][1]), stats[2][1])
        num = jnp.zeros((blk, LANES), F32)
        den = jnp.zeros((blk, LANES), F32)
        for acc_p, m_p, l_p in stats:
            e = jnp.exp2(m_p - mtop)
            num = num + e * acc_p
            den = den + e * l_p
        out = num / den * gather(gf_ref)
        t0 = j * blk
        for r in range(nseg):
            onat_ref[pl.ds(t0 + r, SUBLANES, stride=nseg), :] = out[r * SUBLANES:(r + 1) * SUBLANES]
        o_ref[t0:t0 + blk, :] = onat_ref[t0:t0 + blk, :].astype(BF16)

    for j in range(seg_len // SUBLANES):
        p1_block(j)


def _attn_masks():
    blk = ATTN_BLOCK
    tables = []
    for _, dil in DILATED_PATTERNS:
        na = NSEG // dil
        cq = blk // na
        qi = np.arange(blk)
        kj = np.arange(2 * blk)
        pos_q = na * (qi % cq) + qi // cq
        pos_k = na * (kj % (2 * cq)) + kj // (2 * cq)
        variants = []
        for q_off in (0, blk):
            dist = (pos_q[None, :] + q_off) - pos_k[:, None]
            variants.append(np.where((dist >= 0) & (dist <= blk), 0.0, MASK_VALUE))
        tables.append(np.stack(variants))
    return jnp.asarray(np.stack(tables).astype(np.float32), dtype=BF16)


def _attention(qkvg, seq):
    bsz, nseg, seg_len, _ = qkvg.shape
    n_pairs = D_ATTN // LANES
    blk = ATTN_BLOCK
    eye2 = jnp.asarray(np.tile(np.eye(blk, dtype=np.float32), (2, 1)), dtype=BF16)

    def col_spec(which):
        return pl.BlockSpec((None, nseg, seg_len, LANES), lambda b, hp: (b, 0, 0, which * n_pairs + hp))

    seg_f32 = pltpu.VMEM((nseg, seg_len, LANES), F32)
    pair_f32 = pltpu.VMEM((2, nseg, seg_len, LANES), F32)
    pair_bf16 = pltpu.VMEM((2, nseg, seg_len, LANES), BF16)
    scratch = [pair_f32, seg_f32, pair_f32, seg_f32, pair_bf16, pair_bf16, pltpu.VMEM((seq, LANES), F32)]
    scratch += [seg_f32] * 6
    return pl.pallas_call(
        _attn_kernel,
        grid=(bsz, n_pairs),
        in_specs=[col_spec(0), col_spec(1), col_spec(2), col_spec(3),
                  pl.BlockSpec((2 * blk, blk), lambda b, hp: (0, 0)),
                  pl.BlockSpec((len(DILATED_PATTERNS), 2, 2 * blk, blk), lambda b, hp: (0, 0, 0, 0))],
        out_specs=pl.BlockSpec((None, seq, LANES), lambda b, hp: (b, 0, hp)),
        out_shape=jax.ShapeDtypeStruct((bsz, seq, D_ATTN), BF16),
        scratch_shapes=scratch,
        compiler_params=pltpu.CompilerParams(
            dimension_semantics=("parallel", "parallel"), vmem_limit_bytes=VMEM_LIMIT),
        name="dilated_attn",
    )(qkvg, qkvg, qkvg, qkvg, eye2, _attn_masks())


SSD_T = 512
SSD_PREV = 16
SSD_CHUNK = 256


def _ssd_kernel(x_ref, xprev_ref, nwin_ref, ws_ref, cw_ref, cb_ref, dtb_ref, alog_ref, dskip_ref,
                nw_ref, rexp_ref, tri_ref, y_ref, xbc_ref, z_ref, dt_ref, cx_ref, ilv_ref, state_ref, ydiag_ref):
    step = pl.program_id(1)

    @pl.when(step == 0)
    def _():
        state_ref[...] = jnp.zeros(state_ref.shape, F32)

    un = _rms_normed(x_ref[...], nwin_ref[...]).astype(BF16)
    up = _rms_normed(xprev_ref[...], nwin_ref[...]).astype(BF16)
    keep_prev = (step > 0).astype(F32)

    def proj(c0, width):
        return jnp.dot(un, ws_ref[:, c0:c0 + width], preferred_element_type=F32)

    def front_conv(ci):
        c0 = ci * SSD_CHUNK
        wcols = slice(D_SSM + c0, D_SSM + c0 + SSD_CHUNK)
        prev = jnp.dot(up, ws_ref[:, wcols], preferred_element_type=F32) * keep_prev
        cur = proj(D_SSM + c0, SSD_CHUNK)
        half = SSD_T // 2
        for a in range(SSD_CHUNK // LANES):
            sl = ci * (SSD_CHUNK // LANES) + a
            lanes = slice(a * LANES, (a + 1) * LANES)
            cols = slice(c0 + a * LANES, c0 + (a + 1) * LANES)
            cx_ref[sl, 0:SSD_PREV, :] = prev[:, lanes]
            cx_ref[sl, SSD_PREV:, :] = cur[:, lanes]
            first = SSD_PREV - (CONV_K - 1)
            rows2 = [cx_ref[sl, pl.ds(first + k, half, stride=2), :] for k in range(CONV_K + 1)]
            for parity in range(2):
                conv = cb_ref[:, cols]
                for k in range(CONV_K):
                    conv = conv + cw_ref[k:k + 1, cols] * rows2[k + parity]
                ilv_ref[sl, pl.ds(parity, half, stride=2), :] = _silu(conv)
            xbc_ref[:, cols] = ilv_ref[sl].astype(BF16)

    def front_gate():
        dt_ref[...] = proj(D_SSM + D_CONV, LANES)
        for c0 in range(0, D_SSM, SSD_CHUNK):
            z_ref[:, c0:c0 + SSD_CHUNK] = _silu(proj(c0, SSD_CHUNK)).astype(BF16)

    a_neg = -jnp.exp(alog_ref[...])
    li = lax.broadcasted_iota(jnp.int32, (CHUNK, CHUNK), 0)
    si = lax.broadcasted_iota(jnp.int32, (CHUNK, CHUNK), 1)
    causal = li >= si
    chan = lax.broadcasted_iota(jnp.int32, (CHUNK, D_SSM), 1)
    even_head = (chan // SSM_HEAD_DIM) % 2 == 0
    tri = tri_ref[...]
    rexp = rexp_ref[...]

    def expand(*es):
        parts = []
        for e in es:
            hi = e.astype(BF16)
            parts.append(jnp.concatenate([hi, (e - hi.astype(F32)).astype(BF16)], axis=1))
        out = jnp.dot(jnp.concatenate(parts, axis=0), rexp, preferred_element_type=F32)
        return [out[i * CHUNK:(i + 1) * CHUNK] for i in range(len(es))]

    def scan_chunk(c):
        r0 = c * CHUNK
        xs = xbc_ref[r0:r0 + CHUNK, :D_SSM].astype(F32)

        dt_in = dt_ref[r0:r0 + CHUNK, :] + dtb_ref[...]
        dt = jnp.maximum(dt_in, 0.0) + LN2 * jnp.log2(1.0 + jnp.exp2(-LOG2E * jnp.abs(dt_in)))
        a_dt = dt * a_neg
        a_cs = sum(jnp.dot(tri, part, preferred_element_type=F32) for part in _split3(a_dt))
        a_cs_t = a_cs.T
        a_last = a_cs[CHUNK - 1:CHUNK, :]
        exp_acs_x, dt_x, dstate_x = expand(jnp.exp(a_cs), dt, jnp.exp(a_last - a_cs))
        cdecay_x = exp_acs_x[CHUNK - 1:CHUNK, :]

        xdt = xs * dt_x
        xdt_pair = (jnp.where(even_head, xdt, 0.0).astype(BF16), jnp.where(even_head, 0.0, xdt).astype(BF16))
        xdts_b = (xdt * dstate_x).astype(BF16)

        for g in range(SSM_GROUPS):
            gcols = slice(g * GROUP_W, (g + 1) * GROUP_W)
            b_b = xbc_ref[r0:r0 + CHUNK, D_SSM + g * D_STATE:D_SSM + (g + 1) * D_STATE]
            c_b = xbc_ref[r0:r0 + CHUNK, D_SSM + (SSM_GROUPS + g) * D_STATE:D_SSM + (SSM_GROUPS + g + 1) * D_STATE]
            b_f = b_b.astype(F32)
            cb = lax.dot_general(c_b, b_b, (((1,), (1,)), ((), ())), preferred_element_type=F32)
            s_prev = state_ref[g]
            y_off = jnp.dot(c_b, s_prev.astype(BF16), preferred_element_type=F32) * exp_acs_x[:, gcols]
            new_state = jnp.dot(b_f.T.astype(BF16), xdts_b[:, gcols], preferred_element_type=F32)
            state_ref[g] = s_prev * cdecay_x[:, gcols] + new_state
            for hp in range(HEADS_PER_GROUP // 2):
                pcols = slice(g * GROUP_W + hp * LANES, g * GROUP_W + (hp + 1) * LANES)
                mats = []
                for e in range(2):
                    h = g * HEADS_PER_GROUP + 2 * hp + e
                    seg = a_cs[:, h:h + 1] - a_cs_t[h:h + 1, :]
                    decay = jnp.exp(jnp.where(causal, seg, -jnp.inf))
                    mats.append((cb * decay).astype(BF16))
                ydiag_ref[:, pcols] = jnp.dot(jnp.concatenate(mats, axis=1),
                                              jnp.concatenate([xp[:, pcols] for xp in xdt_pair], axis=0),
                                              preferred_element_type=F32)
            y = ydiag_ref[:, gcols] + y_off + dskip_ref[:, gcols] * xs[:, gcols]
            yz = y * z_ref[r0:r0 + CHUNK, gcols].astype(F32)
            ms = jnp.mean(yz * yz, axis=-1, keepdims=True)
            y_ref[r0:r0 + CHUNK, gcols] = (yz * lax.rsqrt(ms + NORM_EPS) * nw_ref[:, gcols]).astype(BF16)

    for ci in range(D_CONV // SSD_CHUNK):
        front_conv(ci)
    front_gate()
    for c in range(SSD_T // CHUNK):
        scan_chunk(c)


def _ssd(x, norm_w, w_ssm, conv_w, conv_b, dt_bias, a_log, d_skip, ssm_norm_w):
    bsz, seq, _ = x.shape
    prev_per_tile = SSD_T // SSD_PREV
    pad = LANES - SSM_HEADS
    dtb = jnp.pad(dt_bias, (0, pad))[None, :]
    alog = jnp.pad(a_log, (0, pad))[None, :]
    dskip = jnp.repeat(d_skip, SSM_HEAD_DIM)[None, :]
    rexp = (np.arange(LANES)[:, None] == (np.arange(D_SSM)[None, :] // SSM_HEAD_DIM))
    rexp = jnp.asarray(np.tile(rexp.astype(np.float32), (2, 1)), dtype=BF16)
    tri = jnp.asarray(np.tril(np.ones((CHUNK, CHUNK), np.float32)), dtype=BF16)

    def const(shape):
        return pl.BlockSpec(shape, lambda b, i: (0,) * len(shape))

    return pl.pallas_call(
        _ssd_kernel,
        grid=(bsz, seq // SSD_T),
        in_specs=[
            pl.BlockSpec((None, SSD_T, D_MODEL), lambda b, i: (b, i, 0)),
            pl.BlockSpec((None, SSD_PREV, D_MODEL), lambda b, i: (b, jnp.maximum(i * prev_per_tile - 1, 0), 0)),
            const((1, D_MODEL)),
            pl.BlockSpec((D_MODEL, D_SSM_PAD), lambda b, i: (0, 0), pipeline_mode=pl.Buffered(1)),
            const((CONV_K, D_CONV)), const((1, D_CONV)),
            const((1, LANES)), const((1, LANES)),
            const((1, D_SSM)), const((1, D_SSM)), const((2 * LANES, D_SSM)), const((CHUNK, CHUNK)),
        ],
        out_specs=pl.BlockSpec((None, SSD_T, D_SSM), lambda b, i: (b, i, 0)),
        out_shape=jax.ShapeDtypeStruct((bsz, seq, D_SSM), BF16),
        scratch_shapes=[
            pltpu.VMEM((SSD_T, D_CONV), BF16),
            pltpu.VMEM((SSD_T, D_SSM), BF16),
            pltpu.VMEM((SSD_T, LANES), F32),
            pltpu.VMEM((D_CONV // LANES, SSD_PREV + SSD_T, LANES), F32),
            pltpu.VMEM((D_CONV // LANES, SSD_T, LANES), F32),
            pltpu.VMEM((SSM_GROUPS, D_STATE, GROUP_W), F32),
            pltpu.VMEM((CHUNK, D_SSM), F32),
        ],
        compiler_params=pltpu.CompilerParams(
            dimension_semantics=("parallel", "arbitrary"), vmem_limit_bytes=VMEM_LIMIT),
        name="ssd",
    )(x, x, norm_w, w_ssm, conv_w, conv_b, dtb, alog, dskip, ssm_norm_w[None, :], rexp, tri)


OUT_TM = 1024
OUT_PIECE = 256


def _outproj_kernel(attn_ref, y_ref, x_ref, w_ref, nw_ref, o_ref):
    for r0 in range(0, OUT_TM, OUT_PIECE):
        rows = slice(r0, r0 + OUT_PIECE)
        out = jnp.dot(attn_ref[rows, :], w_ref[:D_ATTN, :], preferred_element_type=F32)
        out = out + jnp.dot(y_ref[rows, :], w_ref[D_ATTN:, :], preferred_element_type=F32)
        ms = jnp.mean(out * out, axis=-1, keepdims=True)
        o_ref[rows, :] = x_ref[rows, :] + out * lax.rsqrt(ms + NORM_EPS) * nw_ref[...]


def _outproj(attn2, y2, x2, w_out_b, norm_w):
    m = x2.shape[0]
    return pl.pallas_call(
        _outproj_kernel,
        grid=(m // OUT_TM,),
        in_specs=[
            pl.BlockSpec((OUT_TM, D_ATTN), lambda i: (i, 0)),
            pl.BlockSpec((OUT_TM, D_SSM), lambda i: (i, 0)),
            pl.BlockSpec((OUT_TM, D_MODEL), lambda i: (i, 0)),
            pl.BlockSpec((D_ATTN + D_SSM, D_MODEL), lambda i: (0, 0)),
            pl.BlockSpec((1, D_MODEL), lambda i: (0, 0)),
        ],
        out_specs=pl.BlockSpec((OUT_TM, D_MODEL), lambda i: (i, 0)),
        out_shape=jax.ShapeDtypeStruct((m, D_MODEL), F32),
        compiler_params=pltpu.CompilerParams(
            dimension_semantics=("parallel",), vmem_limit_bytes=VMEM_LIMIT),
        name="outproj",
    )(attn2, y2, x2, w_out_b, norm_w)


def _layer(hid, norm_pre_w, w_in, conv_w, conv_b, dt_bias, a_log, d_skip, ssm_norm_w, w_out, norm_post_w):
    bsz, seq, _ = hid.shape
    assert hid.shape[-1] == D_MODEL and w_in.shape == (D_MODEL, D_IN_PROJ)
    assert seq == 2 * ATTN_BLOCK * NSEG and seq % SSD_T == 0 and (bsz * seq) % OUT_TM == 0
    w_ssm = jnp.pad(w_in[:, D_QKVG:], ((0, 0), (0, D_SSM_PAD - (D_IN_PROJ - D_QKVG)))).astype(BF16)
    q_scale = ATTN_HEAD_DIM ** -0.5 * LOG2E
    col_scale = np.where(np.arange(D_QKVG) < D_ATTN, q_scale, 1.0).astype(np.float32)
    w_attn = (w_in[:, :D_QKVG] * col_scale).astype(BF16)
    qkvg = _inproj(hid, norm_pre_w[None, :], w_attn)
    attn = _attention(qkvg, seq)
    y = _ssd(hid, norm_pre_w[None, :], w_ssm, conv_w, conv_b[None, :], dt_bias, a_log, d_skip, ssm_norm_w)
    x2 = hid.reshape(bsz * seq, D_MODEL)
    out = _outproj(attn.reshape(bsz * seq, D_ATTN), y.reshape(bsz * seq, D_SSM), x2,
                   w_out.astype(BF16), norm_post_w[None, :])
    return out.reshape(bsz, seq, D_MODEL)


def kernel(x, norm_pre_w, w_in, conv_w, conv_b, dt_bias, a_log, d_skip, ssm_norm_w, w_out, norm_post_w):
    hid = x
    for layer in range(norm_pre_w.shape[0]):
        hid = _layer(hid, norm_pre_w[layer], w_in[layer], conv_w[layer], conv_b[layer],
                     dt_bias[layer], a_log[layer], d_skip[layer], ssm_norm_w[layer],
                     w_out[layer], norm_post_w[layer])
    return hid
```

```python
import jax
import jax.numpy as jnp
import numpy as np
from jax import lax
from jax.experimental import pallas as pl
from jax.experimental.pallas import tpu as pltpu

D_MODEL = 1024
ATTN_HEADS = 16
ATTN_HEAD_DIM = 64
D_ATTN = ATTN_HEADS * ATTN_HEAD_DIM
DILATED_PATTERNS = ((128, 1), (512, 4), (2048, 16))
ATTN_BLOCK = 128
D_SSM = 1024
SSM_HEAD_DIM = 64
SSM_HEADS = D_SSM // SSM_HEAD_DIM
SSM_GROUPS = 2
D_STATE = 128
CONV_K = 4
CHUNK = 128
D_CONV = D_SSM + 2 * SSM_GROUPS * D_STATE
D_QKVG = 4 * D_ATTN
D_IN_PROJ = D_QKVG + D_SSM + D_CONV + SSM_HEADS
NORM_EPS = 1e-6

LANES = 128
SUBLANES = 8
D_SSM_PAD = D_SSM + D_CONV + LANES
GROUP_W = D_SSM // SSM_GROUPS
HEADS_PER_GROUP = SSM_HEADS // SSM_GROUPS
MASK_VALUE = -1e30
LOG2E = float(np.log2(np.e))
LN2 = float(np.log(2.0))
VMEM_LIMIT = 56 * 1024 * 1024

NSEG = max(d for _, d in DILATED_PATTERNS)

F32 = jnp.float32
BF16 = jnp.bfloat16


def _silu(x):
    h = 0.5 * x
    return h + h * jnp.tanh(h)


def _split3(x):
    hi = x.astype(BF16)
    r1 = x - hi.astype(F32)
    mid = r1.astype(BF16)
    lo = (r1 - mid.astype(F32)).astype(BF16)
    return hi, mid, lo


def _rms_normed(xf, w):
    ms = jnp.mean(xf * xf, axis=-1, keepdims=True)
    return xf * lax.rsqrt(ms + NORM_EPS) * w


IN_CHUNK = 512
IN_TM = 512
IN_ROWS = IN_TM // NSEG


def _inproj_kernel(x_ref, nw_ref, wa_ref, qkvg_ref, slab_ref, ur_ref):
    uf = _rms_normed(x_ref[...], nw_ref[...])

    n_slabs = D_MODEL // LANES
    for c in range(n_slabs):
        slab_ref[c] = uf[:, c * LANES:(c + 1) * LANES]
    for c in range(n_slabs):
        for r in range(NSEG):
            ur_ref[r * IN_ROWS:(r + 1) * IN_ROWS, c * LANES:(c + 1) * LANES] = (
                slab_ref[c, pl.ds(r, IN_ROWS, stride=NSEG), :].astype(BF16))

    for c0 in reversed(range(0, D_QKVG, IN_CHUNK)):
        res = jnp.dot(ur_ref[...], wa_ref[:, c0:c0 + IN_CHUNK], preferred_element_type=F32)
        if c0 >= 3 * D_ATTN:
            res = _silu(res)
        res = res.astype(BF16)
        for r in range(NSEG):
            qkvg_ref[r, :, c0:c0 + IN_CHUNK] = res[r * IN_ROWS:(r + 1) * IN_ROWS, :]


def _inproj(x, norm_w, w_attn):
    bsz, seq, _ = x.shape
    seg_len = seq // NSEG
    return pl.pallas_call(
        _inproj_kernel,
        grid=(bsz, seq // IN_TM),
        in_specs=[
            pl.BlockSpec((None, IN_TM, D_MODEL), lambda b, k: (b, k, 0)),
            pl.BlockSpec((1, D_MODEL), lambda b, k: (0, 0)),
            pl.BlockSpec((D_MODEL, D_QKVG), lambda b, k: (0, 0), pipeline_mode=pl.Buffered(1)),
        ],
        out_specs=pl.BlockSpec((None, NSEG, IN_ROWS, D_QKVG), lambda b, k: (b, 0, k, 0)),
        out_shape=jax.ShapeDtypeStruct((bsz, NSEG, seg_len, D_QKVG), BF16),
        scratch_shapes=[
            pltpu.VMEM((D_MODEL // LANES, IN_TM, LANES), F32),
            pltpu.VMEM((IN_TM, D_MODEL), BF16),
        ],
        compiler_params=pltpu.CompilerParams(
            dimension_semantics=("parallel", "parallel"), vmem_limit_bytes=VMEM_LIMIT),
        name="inproj",
    )(x, norm_w, w_attn)


def _attn_kernel(q_ref, k_ref, v_ref, g_ref, eye_ref, biast_ref, o_ref,
                 qf_ref, kf_ref, vf_ref, gf_ref, qm_ref, vm_ref, onat_ref, *stat_refs):
    acc_refs, m_refs, l_refs = stat_refs[0:2], stat_refs[2:4], stat_refs[4:6]
    nseg, seg_len, _ = q_ref.shape
    blk = ATTN_BLOCK
    head0 = lax.broadcasted_iota(jnp.int32, (blk, LANES), 1) < ATTN_HEAD_DIM
    head0_seg = lax.broadcasted_iota(jnp.int32, (seg_len, LANES), 1) < ATTN_HEAD_DIM
    head0_win = lax.broadcasted_iota(jnp.int32, (2 * blk, LANES), 1) < ATTN_HEAD_DIM
    ones_h = (jnp.where(head0_win, 1.0, 0.0).astype(BF16), jnp.where(head0_win, 0.0, 1.0).astype(BF16))

    def prep(r):
        gf_ref[r] = g_ref[r].astype(F32)
        qf = q_ref[r].astype(F32)
        vf = v_ref[r].astype(F32)
        for h in range(2):
            keep = head0_seg if h == 0 else jnp.logical_not(head0_seg)
            qh = jnp.where(keep, qf, 0.0)
            vh = jnp.where(keep, vf, 0.0)
            qf_ref[h, r] = qh
            vf_ref[h, r] = vh
            qm_ref[h, r] = qh.astype(BF16)
            vm_ref[h, r] = vh.astype(BF16)
        kf_ref[r] = k_ref[r].astype(F32)

    def attend(q01, kw, v0, v1, bias_t):
        qa = jnp.concatenate([q01, eye_ref[...]], axis=1)
        ka = jnp.concatenate([kw, bias_t], axis=1)
        s = lax.dot_general(qa, ka, (((1,), (1,)), ((), ())), preferred_element_type=F32)
        m = jnp.max(s, axis=-1, keepdims=True)
        pr = jnp.exp2(s - m).astype(BF16)
        lhs = jnp.concatenate([pr[:blk], pr[blk:]], axis=1)
        rhs = jnp.concatenate([jnp.concatenate([v0, ones_h[0]], axis=1),
                               jnp.concatenate([v1, ones_h[1]], axis=1)], axis=0)
        oa = jnp.dot(lhs, rhs, preferred_element_type=F32)
        mb = jnp.broadcast_to(m, (2 * blk, LANES))
        return oa[:, :LANES], jnp.where(head0, mb[:blk], mb[blk:]), oa[:, LANES:]

    dil2 = DILATED_PATTERNS[1][1]
    na = nseg // dil2
    cq = blk // na
    assert seg_len == 2 * blk and DILATED_PATTERNS[2][1] == nseg


    def p3_block(r, jb):
        rows = slice(jb * blk, (jb + 1) * blk)
        q01 = jnp.concatenate([qm_ref[0, r, rows, :], qm_ref[1, r, rows, :]], axis=0)
        acc, mx, den = attend(q01, k_ref[r], vm_ref[0, r], vm_ref[1, r], biast_ref[2, jb])
        acc_refs[1][r, rows, :] = acc
        m_refs[1][r, rows, :] = mx
        l_refs[1][r, rows, :] = den

    for r in range(nseg):
        prep(r)
        p3_block(r, 0)
        p3_block(r, 1)

    def p2_block(r4, j):
        iq, ik = j * cq, max(j - 1, 0) * cq
        segs = [dil2 * a + r4 for a in range(na)]
        q01 = jnp.concatenate([qm_ref[h, sg, iq:iq + cq, :] for h in range(2) for sg in segs], axis=0)
        kw = jnp.concatenate([k_ref[sg, ik:ik + 2 * cq, :] for sg in segs], axis=0)
        v0, v1 = [jnp.concatenate([vm_ref[h, sg, ik:ik + 2 * cq, :] for sg in segs], axis=0) for h in range(2)]
        acc, mx, den = attend(q01, kw, v0, v1, biast_ref[1, min(j, 1)])
        for a, sg in enumerate(segs):
            rows = slice(a * cq, (a + 1) * cq)
            acc_refs[0][sg, iq:iq + cq, :] = acc[rows]
            m_refs[0][sg, iq:iq + cq, :] = mx[rows]
            l_refs[0][sg, iq:iq + cq, :] = den[rows]

    for r4 in range(dil2):
        for j in range(seg_len // cq):
            p2_block(r4, j)

    def p1_block(j):
        iq, ik = j * SUBLANES, max(j - 1, 0) * SUBLANES

        def gather(ref, *lead, width=SUBLANES, start=iq):
            return jnp.concatenate([ref[lead + (r, slice(start, start + width), slice(None))] for r in range(nseg)],
                                   axis=0)

        q01 = jnp.concatenate([gather(qf_ref, 0), gather(qf_ref, 1)], axis=0).astype(BF16)
        kw = gather(kf_ref, width=2 * SUBLANES, start=ik).astype(BF16)
        v0 = gather(vf_ref, 0, width=2 * SUBLANES, start=ik).astype(BF16)
        v1 = gather(vf_ref, 1, width=2 * SUBLANES, start=ik).astype(BF16)
        stats = [attend(q01, kw, v0, v1, biast_ref[0, min(j, 1)])]
        stats += [(gather(acc_refs[p]), gather(m_refs[p]), gather(l_refs[p])) for p in range(2)]
        mtop = jnp.maximum(jnp.maximum(stats[0][1], stats[1][1]), stats[2][1])
        num = jnp.zeros((blk, LANES), F32)
        den = jnp.zeros((blk, LANES), F32)
        for acc_p, m_p, l_p in stats:
            e = jnp.exp2(m_p - mtop)
            num = num + e * acc_p
            den = den + e * l_p
        out = num / den * gather(gf_ref)
        t0 = j * blk
        for r in range(nseg):
            onat_ref[pl.ds(t0 + r, SUBLANES, stride=nseg), :] = out[r * SUBLANES:(r + 1) * SUBLANES]
        o_ref[t0:t0 + blk, :] = onat_ref[t0:t0 + blk, :].astype(BF16)

    for j in range(seg_len // SUBLANES):
        p1_block(j)


def _attn_masks():
    blk = ATTN_BLOCK
    tables = []
    for _, dil in DILATED_PATTERNS:
        na = NSEG // dil
        cq = blk // na
        qi = np.arange(blk)
        kj = np.arange(2 * blk)
        pos_q = na * (qi % cq) + qi // cq
        pos_k = na * (kj % (2 * cq)) + kj // (2 * cq)
        variants = []
        for q_off in (0, blk):
            dist = (pos_q[None, :] + q_off) - pos_k[:, None]
            variants.append(np.where((dist >= 0) & (dist <= blk), 0.0, MASK_VALUE))
        tables.append(np.stack(variants))
    return jnp.asarray(np.stack(tables).astype(np.float32), dtype=BF16)


def _attention(qkvg, seq):
    bsz, nseg, seg_len, _ = qkvg.shape
    n_pairs = D_ATTN // LANES
    blk = ATTN_BLOCK
    eye2 = jnp.asarray(np.tile(np.eye(blk, dtype=np.float32), (2, 1)), dtype=BF16)

    def col_spec(which):
        return pl.BlockSpec((None, nseg, seg_len, LANES), lambda b, hp: (b, 0, 0, which * n_pairs + hp))

    seg_f32 = pltpu.VMEM((nseg, seg_len, LANES), F32)
    pair_f32 = pltpu.VMEM((2, nseg, seg_len, LANES), F32)
    pair_bf16 = pltpu.VMEM((2, nseg, seg_len, LANES), BF16)
    scratch = [pair_f32, seg_f32, pair_f32, seg_f32, pair_bf16, pair_bf16, pltpu.VMEM((seq, LANES), F32)]
    scratch += [seg_f32] * 6
    return pl.pallas_call(
        _attn_kernel,
        grid=(bsz, n_pairs),
        in_specs=[col_spec(0), col_spec(1), col_spec(2), col_spec(3),
                  pl.BlockSpec((2 * blk, blk), lambda b, hp: (0, 0)),
                  pl.BlockSpec((len(DILATED_PATTERNS), 2, 2 * blk, blk), lambda b, hp: (0, 0, 0, 0))],
        out_specs=pl.BlockSpec((None, seq, LANES), lambda b, hp: (b, 0, hp)),
        out_shape=jax.ShapeDtypeStruct((bsz, seq, D_ATTN), BF16),
        scratch_shapes=scratch,
        compiler_params=pltpu.CompilerParams(
            dimension_semantics=("parallel", "parallel"), vmem_limit_bytes=VMEM_LIMIT),
        name="dilated_attn",
    )(qkvg, qkvg, qkvg, qkvg, eye2, _attn_masks())


SSD_T = 1024
SSD_PREV = 16
SSD_CHUNK = 256


def _ssd_kernel(x_ref, xprev_ref, nwin_ref, ws_ref, cw_ref, cb_ref, dtb_ref, alog_ref, dskip_ref,
                nw_ref, rexp_ref, tri_ref, y_ref, xbc_ref, z_ref, dt_ref, cx_ref, ilv_ref, state_ref, ydiag_ref):
    step = pl.program_id(1)

    @pl.when(step == 0)
    def _():
        state_ref[...] = jnp.zeros(state_ref.shape, F32)

    un = _rms_normed(x_ref[...], nwin_ref[...]).astype(BF16)
    up = _rms_normed(xprev_ref[...], nwin_ref[...]).astype(BF16)
    keep_prev = (step > 0).astype(F32)

    def proj(c0, width):
        return jnp.dot(un, ws_ref[:, c0:c0 + width], preferred_element_type=F32)

    def front_conv(ci):
        c0 = ci * SSD_CHUNK
        wcols = slice(D_SSM + c0, D_SSM + c0 + SSD_CHUNK)
        prev = jnp.dot(up, ws_ref[:, wcols], preferred_element_type=F32) * keep_prev
        cur = proj(D_SSM + c0, SSD_CHUNK)
        half = SSD_T // 2
        for a in range(SSD_CHUNK // LANES):
            sl = ci * (SSD_CHUNK // LANES) + a
            lanes = slice(a * LANES, (a + 1) * LANES)
            cols = slice(c0 + a * LANES, c0 + (a + 1) * LANES)
            cx_ref[sl, 0:SSD_PREV, :] = prev[:, lanes]
            cx_ref[sl, SSD_PREV:, :] = cur[:, lanes]
            first = SSD_PREV - (CONV_K - 1)
            rows2 = [cx_ref[sl, pl.ds(first + k, half, stride=2), :] for k in range(CONV_K + 1)]
            for parity in range(2):
                conv = cb_ref[:, cols]
                for k in range(CONV_K):
                    conv = conv + cw_ref[k:k + 1, cols] * rows2[k + parity]
                ilv_ref[sl, pl.ds(parity, half, stride=2), :] = _silu(conv)
            xbc_ref[:, cols] = ilv_ref[sl].astype(BF16)

    def front_gate():
        dt_ref[...] = proj(D_SSM + D_CONV, LANES)
        for c0 in range(0, D_SSM, SSD_CHUNK):
            z_ref[:, c0:c0 + SSD_CHUNK] = _silu(proj(c0, SSD_CHUNK)).astype(BF16)

    a_neg = -jnp.exp(alog_ref[...])
    li = lax.broadcasted_iota(jnp.int32, (CHUNK, CHUNK), 0)
    si = lax.broadcasted_iota(jnp.int32, (CHUNK, CHUNK), 1)
    causal = li >= si
    chan = lax.broadcasted_iota(jnp.int32, (CHUNK, D_SSM), 1)
    even_head = (chan // SSM_HEAD_DIM) % 2 == 0
    tri = tri_ref[...]
    rexp = rexp_ref[...]

    def expand(*es):
        parts = []
        for e in es:
            hi = e.astype(BF16)
            parts.append(jnp.concatenate([hi, (e - hi.astype(F32)).astype(BF16)], axis=1))
        out = jnp.dot(jnp.concatenate(parts, axis=0), rexp, preferred_element_type=F32)
        return [out[i * CHUNK:(i + 1) * CHUNK] for i in range(len(es))]

    def scan_chunk(c):
        r0 = c * CHUNK
        xs = xbc_ref[r0:r0 + CHUNK, :D_SSM].astype(F32)

        dt_in = dt_ref[r0:r0 + CHUNK, :] + dtb_ref[...]
        dt = jnp.maximum(dt_in, 0.0) + LN2 * jnp.log2(1.0 + jnp.exp2(-LOG2E * jnp.abs(dt_in)))
        a_dt = dt * a_neg
        a_cs = sum(jnp.dot(tri, part, preferred_element_type=F32) for part in _split3(a_dt))
        a_cs_t = a_cs.T
        a_last = a_cs[CHUNK - 1:CHUNK, :]
        exp_acs_x, dt_x, dstate_x = expand(jnp.exp(a_cs), dt, jnp.exp(a_last - a_cs))
        cdecay_x = exp_acs_x[CHUNK - 1:CHUNK, :]

        xdt = xs * dt_x
        xdt_pair = (jnp.where(even_head, xdt, 0.0).astype(BF16), jnp.where(even_head, 0.0, xdt).astype(BF16))
        xdts_b = (xdt * dstate_x).astype(BF16)

        for g in range(SSM_GROUPS):
            gcols = slice(g * GROUP_W, (g + 1) * GROUP_W)
            b_b = xbc_ref[r0:r0 + CHUNK, D_SSM + g * D_STATE:D_SSM + (g + 1) * D_STATE]
            c_b = xbc_ref[r0:r0 + CHUNK, D_SSM + (SSM_GROUPS + g) * D_STATE:D_SSM + (SSM_GROUPS + g + 1) * D_STATE]
            b_f = b_b.astype(F32)
            cb = lax.dot_general(c_b, b_b, (((1,), (1,)), ((), ())), preferred_element_type=F32)
            s_prev = state_ref[g]
            y_off = jnp.dot(c_b, s_prev.astype(BF16), preferred_element_type=F32) * exp_acs_x[:, gcols]
            new_state = jnp.dot(b_f.T.astype(BF16), xdts_b[:, gcols], preferred_element_type=F32)
            state_ref[g] = s_prev * cdecay_x[:, gcols] + new_state
            for hp in range(HEADS_PER_GROUP // 2):
                pcols = slice(g * GROUP_W + hp * LANES, g * GROUP_W + (hp + 1) * LANES)
                mats = []
                for e in range(2):
                    h = g * HEADS_PER_GROUP + 2 * hp + e
                    seg = a_cs[:, h:h + 1] - a_cs_t[h:h + 1, :]
                    decay = jnp.exp(jnp.where(causal, seg, -jnp.inf))
                    mats.append((cb * decay).astype(BF16))
                ydiag_ref[:, pcols] = jnp.dot(jnp.concatenate(mats, axis=1),
                                              jnp.concatenate([xp[:, pcols] for xp in xdt_pair], axis=0),
                                              preferred_element_type=F32)
            y = ydiag_ref[:, gcols] + y_off + dskip_ref[:, gcols] * xs[:, gcols]
            yz = y * z_ref[r0:r0 + CHUNK, gcols].astype(F32)
            ms = jnp.mean(yz * yz, axis=-1, keepdims=True)
            y_ref[r0:r0 + CHUNK, gcols] = (yz * lax.rsqrt(ms + NORM_EPS) * nw_ref[:, gcols]).astype(BF16)

    for ci in range(D_CONV // SSD_CHUNK):
        front_conv(ci)
    front_gate()
    for c in range(SSD_T // CHUNK):
        scan_chunk(c)


def _ssd(x, norm_w, w_ssm, conv_w, conv_b, dt_bias, a_log, d_skip, ssm_norm_w):
    bsz, seq, _ = x.shape
    prev_per_tile = SSD_T // SSD_PREV
    pad = LANES - SSM_HEADS
    dtb = jnp.pad(dt_bias, (0, pad))[None, :]
    alog = jnp.pad(a_log, (0, pad))[None, :]
    dskip = jnp.repeat(d_skip, SSM_HEAD_DIM)[None, :]
    rexp = (np.arange(LANES)[:, None] == (np.arange(D_SSM)[None, :] // SSM_HEAD_DIM))
    rexp = jnp.asarray(np.tile(rexp.astype(np.float32), (2, 1)), dtype=BF16)
    tri = jnp.asarray(np.tril(np.ones((CHUNK, CHUNK), np.float32)), dtype=BF16)

    def const(shape):
        return pl.BlockSpec(shape, lambda b, i: (0,) * len(shape))

    return pl.pallas_call(
        _ssd_kernel,
        grid=(bsz, seq // SSD_T),
        in_specs=[
            pl.BlockSpec((None, SSD_T, D_MODEL), lambda b, i: (b, i, 0)),
            pl.BlockSpec((None, SSD_PREV, D_MODEL), lambda b, i: (b, jnp.maximum(i * prev_per_tile - 1, 0), 0)),
            const((1, D_MODEL)),
            pl.BlockSpec((D_MODEL, D_SSM_PAD), lambda b, i: (0, 0), pipeline_mode=pl.Buffered(1)),
            const((CONV_K, D_CONV)), const((1, D_CONV)),
            const((1, LANES)), const((1, LANES)),
            const((1, D_SSM)), const((1, D_SSM)), const((2 * LANES, D_SSM)), const((CHUNK, CHUNK)),
        ],
        out_specs=pl.BlockSpec((None, SSD_T, D_SSM), lambda b, i: (b, i, 0)),
        out_shape=jax.ShapeDtypeStruct((bsz, seq, D_SSM), BF16),
        scratch_shapes=[
            pltpu.VMEM((SSD_T, D_CONV), BF16),
            pltpu.VMEM((SSD_T, D_SSM), BF16),
            pltpu.VMEM((SSD_T, LANES), F32),
            pltpu.VMEM((D_CONV // LANES, SSD_PREV + SSD_T, LANES), F32),
            pltpu.VMEM((D_CONV // LANES, SSD_T, LANES), F32),
            pltpu.VMEM((SSM_GROUPS, D_STATE, GROUP_W), F32),
            pltpu.VMEM((CHUNK, D_SSM), F32),
        ],
        compiler_params=pltpu.CompilerParams(
            dimension_semantics=("parallel", "arbitrary"), vmem_limit_bytes=VMEM_LIMIT),
        name="ssd",
    )(x, x, norm_w, w_ssm, conv_w, conv_b, dtb, alog, dskip, ssm_norm_w[None, :], rexp, tri)


OUT_TM = 1024
OUT_PIECE = 1024


def _outproj_kernel(attn_ref, y_ref, x_ref, w_ref, nw_ref, o_ref):
    for r0 in range(0, OUT_TM, OUT_PIECE):
        rows = slice(r0, r0 + OUT_PIECE)
        out = jnp.dot(attn_ref[rows, :], w_ref[:D_ATTN, :], preferred_element_type=F32)
        out = out + jnp.dot(y_ref[rows, :], w_ref[D_ATTN:, :], preferred_element_type=F32)
        ms = jnp.mean(out * out, axis=-1, keepdims=True)
        o_ref[rows, :] = x_ref[rows, :] + out * lax.rsqrt(ms + NORM_EPS) * nw_ref[...]


def _outproj(attn2, y2, x2, w_out_b, norm_w):
    m = x2.shape[0]
    return pl.pallas_call(
        _outproj_kernel,
        grid=(m // OUT_TM,),
        in_specs=[
            pl.BlockSpec((OUT_TM, D_ATTN), lambda i: (i, 0)),
            pl.BlockSpec((OUT_TM, D_SSM), lambda i: (i, 0)),
            pl.BlockSpec((OUT_TM, D_MODEL), lambda i: (i, 0)),
            pl.BlockSpec((D_ATTN + D_SSM, D_MODEL), lambda i: (0, 0)),
            pl.BlockSpec((1, D_MODEL), lambda i: (0, 0)),
        ],
        out_specs=pl.BlockSpec((OUT_TM, D_MODEL), lambda i: (i, 0)),
        out_shape=jax.ShapeDtypeStruct((m, D_MODEL), F32),
        compiler_params=pltpu.CompilerParams(
            dimension_semantics=("parallel",), vmem_limit_bytes=VMEM_LIMIT),
        name="outproj",
    )(attn2, y2, x2, w_out_b, norm_w)


def _layer(hid, norm_pre_w, w_in, conv_w, conv_b, dt_bias, a_log, d_skip, ssm_norm_w, w_out, norm_post_w):
    bsz, seq, _ = hid.shape
    assert hid.shape[-1] == D_MODEL and w_in.shape == (D_MODEL, D_IN_PROJ)
    assert seq == 2 * ATTN_BLOCK * NSEG and seq % SSD_T == 0 and (bsz * seq) % OUT_TM == 0
    w_ssm = jnp.pad(w_in[:, D_QKVG:], ((0, 0), (0, D_SSM_PAD - (D_IN_PROJ - D_QKVG)))).astype(BF16)
    q_scale = ATTN_HEAD_DIM ** -0.5 * LOG2E
    col_scale = np.where(np.arange(D_QKVG) < D_ATTN, q_scale, 1.0).astype(np.float32)
    w_attn = (w_in[:, :D_QKVG] * col_scale).astype(BF16)
    qkvg = _inproj(hid, norm_pre_w[None, :], w_attn)
    attn = _attention(qkvg, seq)
    y = _ssd(hid, norm_pre_w[None, :], w_ssm, conv_w, conv_b[None, :], dt_bias, a_log, d_skip, ssm_norm_w)
    x2 = hid.reshape(bsz * seq, D_MODEL)
    out = _outproj(attn.reshape(bsz * seq, D_ATTN), y.reshape(bsz * seq, D_SSM), x2,
                   w_out.astype(BF16), norm_post_w[None, :])
    return out.reshape(bsz, seq, D_MODEL)


def kernel(x, norm_pre_w, w_in, conv_w, conv_b, dt_bias, a_log, d_skip, ssm_norm_w, w_out, norm_post_w):
    hid = x
    for layer in range(norm_pre_w.shape[0]):
        hid = _layer(hid, norm_pre_w[layer], w_in[layer], conv_w[layer], conv_b[layer],
                     dt_bias[layer], a_log[layer], d_skip[layer], ssm_norm_w[layer],
                     w_out[layer], norm_post_w[layer])
    return hid
```
